```python
import jax, jax.numpy as jnp
from jax import lax
import numpy as np

D_MODEL = 2048
BATCH = 2
SEQ = 8192
DEPTH = 4

N_MIXERS = 2
N_SB_LAYERS = (DEPTH + 1) // 2
N_SG_LAYERS = DEPTH // 2
SB_HEADS = 16
SB_HEAD_DIM = D_MODEL // SB_HEADS
Q_BLOCK = 128
SG_GROUPS = 16
SG_CHUNK = 128
SG_HALF = D_MODEL
SG_GROUP_DIM = SG_HALF // SG_GROUPS
N_GROUPS = 4
EXPERTS_PER_GROUP = 8
N_EXPERTS = N_GROUPS * EXPERTS_PER_GROUP
TOP_K_IN_GROUP = 2
D_EXPERT = 256
PLE_DIM = 256
EPS = 1e-6

kernel_name = "hybrid_stickbreak_spatialgate_hmoe"


def rmsnorm(x, g):
    xf = x.astype(jnp.float32)
    y = xf * lax.rsqrt(jnp.mean(xf * xf, axis=-1, keepdims=True) + EPS)
    return (y * g.astype(jnp.float32)).astype(x.dtype)


def stick_breaking_attention(h, w_in, q_norm, k_norm, w_out):
    B, S, _ = h.shape
    qkv = h @ w_in
    q, k, v = jnp.split(qkv, 3, axis=-1)
    to_heads = lambda t: t.reshape(B, S, SB_HEADS, SB_HEAD_DIM).transpose(0, 2, 1, 3)
    q = rmsnorm(to_heads(q), q_norm)
    k = rmsnorm(to_heads(k), k_norm)
    v = to_heads(v)
    scale = SB_HEAD_DIM ** -0.5
    n_blk = S // Q_BLOCK
    q_blocks = q.reshape(B, SB_HEADS, n_blk, Q_BLOCK, SB_HEAD_DIM).transpose(2, 0, 1, 3, 4)
    starts = jnp.arange(n_blk, dtype=jnp.int32) * Q_BLOCK
    kf = k.astype(jnp.float32)
    vf = v.astype(jnp.float32)
    key_pos = jnp.arange(S, dtype=jnp.int32)

    def one_block(args):
        q_blk, start = args
        z = jnp.einsum('bhqd,bhsd->bhqs', q_blk.astype(jnp.float32), kf) * scale
        t_pos = start + jnp.arange(Q_BLOCK, dtype=jnp.int32)
        causal = key_pos[None, :] < t_pos[:, None]
        log_keep = jnp.where(causal, jax.nn.log_sigmoid(-z), 0.0)
        log_stick = lax.cumsum(log_keep, axis=3, reverse=True) - log_keep
        weight = jnp.where(causal, jnp.exp(jax.nn.log_sigmoid(z) + log_stick), 0.0)
        return jnp.einsum('bhqs,bhsd->bhqd', weight, vf)

    o = lax.map(one_block, (q_blocks, starts))
    o = o.transpose(1, 0, 3, 2, 4).reshape(B, S, D_MODEL).astype(h.dtype)
    return o @ w_out


def spatial_gating_mlp(h, w_in, v_norm, w_s, b_s, w_out):
    B, S, _ = h.shape
    z = jax.nn.gelu(h @ w_in)
    u, v = jnp.split(z, 2, axis=-1)
    v = rmsnorm(v, v_norm)
    n_chunk = S // SG_CHUNK
    vg = v.reshape(B, n_chunk, SG_CHUNK, SG_GROUPS, SG_GROUP_DIM)
    causal = jnp.tril(jnp.ones((SG_CHUNK, SG_CHUNK), dtype=bool))
    w_c = jnp.where(causal[None], w_s, 0.0)
    mixed = jnp.einsum('gts,bcsgd->bctgd', w_c, vg) + b_s.T[None, None, :, :, None]
    gated = u * mixed.reshape(B, S, SG_HALF)
    return gated @ w_out


def hierarchical_moe(h, w_group, b_group, w_expert, b_expert, w_gate, w_up, w_down):
    B, S, D = h.shape
    hf = h.reshape(B * S, D)
    T = hf.shape[0]
    group_logits = (hf @ w_group + b_group).astype(jnp.float32)
    group_prob = jax.nn.softmax(group_logits, axis=-1)
    g_idx = jnp.argmax(group_logits, axis=-1)
    g_w = jnp.take_along_axis(group_prob, g_idx[:, None], axis=-1)
    exp_logits = (hf @ w_expert + b_expert).astype(jnp.float32).reshape(T, N_GROUPS, EXPERTS_PER_GROUP)
    in_group = jnp.take_along_axis(exp_logits, g_idx[:, None, None], axis=1)[:, 0]
    top_val, top_idx = lax.top_k(in_group, TOP_K_IN_GROUP)
    weights = g_w * jax.nn.softmax(top_val, axis=-1)
    expert_id = g_idx[:, None] * EXPERTS_PER_GROUP + top_idx
    gates = jnp.sum(jax.nn.one_hot(expert_id, N_EXPERTS, dtype=jnp.float32) * weights[..., None], axis=1)
    hidden = jax.nn.silu(jnp.einsum('td,edf->tef', hf, w_gate)) * jnp.einsum('td,edf->tef', hf, w_up)
    y = jnp.einsum('tef,efd->td', hidden * gates[:, :, None].astype(hidden.dtype), w_down)
    return y.reshape(B, S, D)


def per_layer_embedding(x, p_i, norm_in, w_gate, w_proj, norm_out):
    gate = jax.nn.sigmoid(rmsnorm(x, norm_in) @ w_gate)
    e = (p_i @ w_proj) * gate
    return x + rmsnorm(e, norm_out)


def setup_inputs(seed: int = 0) -> dict:
    key = jax.random.key(seed)
    ks = jax.random.split(key, 32)
    f32 = jnp.float32

    def w(k, shape, fan_in):
        return jax.random.normal(k, shape, f32) * fan_in ** -0.5

    def gain(k, shape):
        return 1.0 + 0.02 * jax.random.normal(k, shape, f32)

    D = D_MODEL
    return {
        "x": jax.random.normal(ks[0], (BATCH, SEQ, D), f32),
        "p": jax.random.normal(ks[1], (DEPTH, BATCH, SEQ, PLE_DIM), f32),
        "norm_mix": gain(ks[2], (DEPTH, D)),
        "norm_ffn": gain(ks[3], (DEPTH, D)),
        "sb_w_in": w(ks[4], (N_SB_LAYERS, D, 3 * D), D),
        "sb_q_norm": gain(ks[5], (N_SB_LAYERS, SB_HEAD_DIM)),
        "sb_k_norm": gain(ks[6], (N_SB_LAYERS, SB_HEAD_DIM)),
        "sb_w_out": w(ks[7], (N_SB_LAYERS, D, D), D),
        "sg_w_in": w(ks[8], (N_SG_LAYERS, D, 2 * SG_HALF), D),
        "sg_v_norm": gain(ks[9], (N_SG_LAYERS, SG_HALF)),
        "sg_w_s": 0.5 * w(ks[10], (N_SG_LAYERS, SG_GROUPS, SG_CHUNK, SG_CHUNK), SG_CHUNK),
        "sg_b_s": gain(ks[11], (N_SG_LAYERS, SG_GROUPS, SG_CHUNK)),
        "sg_w_out": w(ks[12], (N_SG_LAYERS, SG_HALF, D), SG_HALF),
        "moe_w_group": w(ks[13], (DEPTH, D, N_GROUPS), D),
        "moe_b_group": 0.01 * jax.random.normal(ks[14], (DEPTH, N_GROUPS), f32),
        "moe_w_expert": w(ks[15], (DEPTH, D, N_EXPERTS), D),
        "moe_b_expert": 0.01 * jax.random.normal(ks[16], (DEPTH, N_EXPERTS), f32),
        "moe_w_gate": w(ks[17], (DEPTH, N_EXPERTS, D, D_EXPERT), D),
        "moe_w_up": w(ks[18], (DEPTH, N_EXPERTS, D, D_EXPERT), D),
        "moe_w_down": w(ks[19], (DEPTH, N_EXPERTS, D_EXPERT, D), D_EXPERT),
        "ple_norm_in": gain(ks[20], (DEPTH, D)),
        "ple_w_gate": w(ks[21], (DEPTH, D, D), D),
        "ple_w_proj": w(ks[22], (DEPTH, PLE_DIM, D), PLE_DIM),
        "ple_norm_out": gain(ks[23], (DEPTH, D)),
    }


def reference(x, p, norm_mix, norm_ffn,
              sb_w_in, sb_q_norm, sb_k_norm, sb_w_out,
              sg_w_in, sg_v_norm, sg_w_s, sg_b_s, sg_w_out,
              moe_w_group, moe_b_group, moe_w_expert, moe_b_expert,
              moe_w_gate, moe_w_up, moe_w_down,
              ple_norm_in, ple_w_gate, ple_w_proj, ple_norm_out):
    for i in range(DEPTH):
        h = rmsnorm(x, norm_mix[i])
        j = i // N_MIXERS
        if i % N_MIXERS == 0:
            x = x + stick_breaking_attention(h, sb_w_in[j], sb_q_norm[j], sb_k_norm[j], sb_w_out[j])
        else:
            x = x + spatial_gating_mlp(h, sg_w_in[j], sg_v_norm[j], sg_w_s[j], sg_b_s[j], sg_w_out[j])
        x = x + hierarchical_moe(rmsnorm(x, norm_ffn[i]), moe_w_group[i], moe_b_group[i],
                                 moe_w_expert[i], moe_b_expert[i],
                                 moe_w_gate[i], moe_w_up[i], moe_w_down[i])
        x = per_layer_embedding(x, p[i], ple_norm_in[i], ple_w_gate[i], ple_w_proj[i], ple_norm_out[i])
    return x
```

```python
import functools
import math

import jax
import jax.numpy as jnp
from jax import lax
from jax.experimental import pallas as pl
from jax.experimental.pallas import tpu as pltpu

F32 = jnp.float32
BF16 = jnp.bfloat16

LANES = 128
ROW_TILES = 16
VMEM_LIMIT_BYTES = 56 * 1024 * 1024
EPS = 1e-6
LOG2E = 1.4426950408889634
INV_LN2 = LOG2E

N_GROUPS = 4
EXPERTS_PER_GROUP = 8
N_EXPERTS = N_GROUPS * EXPERTS_PER_GROUP
ROUTER_GROUP_LANE0 = N_EXPERTS

TM_PROJ = 512
TM_ROW = 256
TQ = 256
TK = 256
TM_EXPERT = 256
TD_MOVE = 512


def _cparams(*sem):
    return pltpu.CompilerParams(dimension_semantics=sem, vmem_limit_bytes=VMEM_LIMIT_BYTES)


def _rms(x, g):
    ms = jnp.mean(x * x, axis=-1, keepdims=True)
    return x * lax.rsqrt(ms + EPS) * g


def _qkv_kernel(x_ref, g_ref, w_ref, cg_ref, o_ref, xn_ref, *, n_norm_tiles):
    j = pl.program_id(1)

    @pl.when(j == 0)
    def _():
        xn_ref[...] = _rms(x_ref[...], g_ref[...]).astype(BF16)

    acc = jnp.dot(xn_ref[...], w_ref[...], preferred_element_type=F32)

    @pl.when(j < n_norm_tiles)
    def _():
        for h in range(acc.shape[1] // LANES):
            sl = slice(h * LANES, (h + 1) * LANES)
            a = acc[:, sl]
            ms = jnp.mean(a * a, axis=-1, keepdims=True)
            o_ref[:, sl] = (a * lax.rsqrt(ms + EPS) * cg_ref[:, sl]).astype(BF16)

    @pl.when(j >= n_norm_tiles)
    def _():
        o_ref[...] = acc.astype(BF16)


def _qkv_proj(x, g, w, colgain, *, tn=1024):
    t, d = x.shape
    n = w.shape[1]
    tm = min(TM_PROJ, t)
    tn = min(tn, d)
    return pl.pallas_call(
        functools.partial(_qkv_kernel, n_norm_tiles=2 * d // tn),
        grid=(t // tm, n // tn),
        in_specs=[
            pl.BlockSpec((tm, d), lambda i, j: (i, 0)),
            pl.BlockSpec((1, d), lambda i, j: (0, 0)),
            pl.BlockSpec((d, tn), lambda i, j: (0, j)),
            pl.BlockSpec((1, tn), lambda i, j: (0, j)),
        ],
        out_specs=pl.BlockSpec((tm, tn), lambda i, j: (i, j)),
        out_shape=jax.ShapeDtypeStruct((t, n), BF16),
        scratch_shapes=[pltpu.VMEM((tm, d), BF16)],
        compiler_params=_cparams("parallel", "arbitrary"),
        name="qkv_proj",
    )(x, g, w, colgain)


def _sg_in_kernel(x_ref, g_ref, w_ref, vg_ref, o_ref, xn_ref):
    j = pl.program_id(1)

    @pl.when(j == 0)
    def _():
        xn_ref[...] = _rms(x_ref[...], g_ref[...]).astype(BF16)

    z = jax.nn.gelu(jnp.dot(xn_ref[...], w_ref[...], preferred_element_type=F32))

    @pl.when(j == 0)
    def _():
        o_ref[...] = z.astype(BF16)

    @pl.when(j == 1)
    def _():
        o_ref[...] = _rms(z, vg_ref[...]).astype(BF16)


def _sg_in_proj(x, g, w, v_gain):
    t, d = x.shape
    tm = min(TM_PROJ, t)
    return pl.pallas_call(
        _sg_in_kernel,
        grid=(t // tm, 2),
        in_specs=[
            pl.BlockSpec((tm, d), lambda i, j: (i, 0)),
            pl.BlockSpec((1, d), lambda i, j: (0, 0)),
            pl.BlockSpec((d, d), lambda i, j: (0, j)),
            pl.BlockSpec((1, d), lambda i, j: (0, 0)),
        ],
        out_specs=pl.BlockSpec((tm, d), lambda i, j: (i, j)),
        out_shape=jax.ShapeDtypeStruct((t, 2 * d), BF16),
        scratch_shapes=[pltpu.VMEM((tm, d), BF16)],
        compiler_params=_cparams("parallel", "arbitrary"),
        name="sg_in_proj",
    )(x, g, w, v_gain)


def _attn_kernel(q_ref, k_ref, v_ref, uu_ref, o_ref):
    qi = pl.program_id(2)
    tq = q_ref.shape[0]
    tk = uu_ref.shape[1]
    q = q_ref[...]

    def tile(kj, carry, acc, causal):
        k = k_ref[pl.ds(pl.multiple_of(kj * tk, tk), tk), :]
        v = v_ref[pl.ds(pl.multiple_of(kj * tk, tk), tk), :]
        z = lax.dot_general(q, k, (((1,), (1,)), ((), ())), preferred_element_type=F32)
        sp = jnp.maximum(z, 0.0) + jnp.log(1.0 + jnp.exp2(-jnp.abs(z))) * INV_LN2
        if causal is not None:
            sp = jnp.where(causal, sp, 0.0)
        hi = sp.astype(BF16)
        lo = (sp - hi.astype(F32)).astype(BF16)
        c = jnp.dot(jnp.concatenate([hi, lo], axis=1), uu_ref[...], preferred_element_type=F32)
        w = jnp.exp2(z - c - carry)
        if causal is not None:
            w = jnp.where(causal, w, 0.0)
        acc = acc + jnp.dot(w.astype(BF16), v, preferred_element_type=F32)
        return carry + c[:, 0:1], acc

    rows = lax.broadcasted_iota(jnp.int32, (tq, tk), 0)
    cols = lax.broadcasted_iota(jnp.int32, (tq, tk), 1)
    carry0 = jnp.zeros((tq, 1), F32)
    acc0 = jnp.zeros((tq, q_ref.shape[1]), F32)
    carry, acc = tile(qi, carry0, acc0, cols < rows)

    def body(it, state):
        return tile(qi - 1 - it, state[0], state[1], None)

    carry, acc = lax.fori_loop(0, qi, body, (carry, acc))
    o_ref[...] = acc.astype(o_ref.dtype)


def _stick_breaking(qkv, heads):
    b, s, d3 = qkv.shape
    d = d3 // 3
    tq = min(TQ, s)
    tk = tq
    j = lax.broadcasted_iota(jnp.int32, (tk, tk), 0)
    c = lax.broadcasted_iota(jnp.int32, (tk, tk), 1)
    u = (j >= c).astype(BF16)
    uu = jnp.concatenate([u, u], axis=0)
    return pl.pallas_call(
        _attn_kernel,
        grid=(b, heads, s // tq),
        in_specs=[
            pl.BlockSpec((None, tq, LANES), lambda bi, h, i: (bi, i, h)),
            pl.BlockSpec((None, s, LANES), lambda bi, h, i: (bi, 0, heads + h)),
            pl.BlockSpec((None, s, LANES), lambda bi, h, i: (bi, 0, 2 * heads + h)),
            pl.BlockSpec((2 * tk, tk), lambda bi, h, i: (0, 0)),
        ],
        out_specs=pl.BlockSpec((None, tq, LANES), lambda bi, h, i: (bi, i, h)),
        out_shape=jax.ShapeDtypeStruct((b, s, d), BF16),
        compiler_params=_cparams("parallel", "parallel", "arbitrary"),
        name="stick_breaking_attention",
    )(qkv, qkv, qkv, uu)


def _split_hi_lo(x):
    hi = x.astype(BF16)
    return hi, (x - hi.astype(F32)).astype(BF16)


def _route_tile(h2, wr_ref, br_ref, ltri_ref, cnt_ref):
    tm = h2.shape[0]
    hi, lo = _split_hi_lo(h2)
    lhs = jnp.concatenate([hi, lo, hi], axis=1)
    logits = jnp.dot(lhs, wr_ref[...], preferred_element_type=F32) + br_ref[...]

    lane = lax.broadcasted_iota(jnp.int32, (tm, LANES), 1)
    neg = jnp.float32(-jnp.inf)
    big = jnp.int32(LANES)

    is_group = (lane >= ROUTER_GROUP_LANE0) & (lane < ROUTER_GROUP_LANE0 + N_GROUPS)
    gl = jnp.where(is_group, logits, neg)
    gmax = jnp.max(gl, axis=-1, keepdims=True)
    g_idx = jnp.min(jnp.where(gl == gmax, lane - ROUTER_GROUP_LANE0, big), axis=-1, keepdims=True)
    g_w = 1.0 / jnp.sum(jnp.where(is_group, jnp.exp(gl - gmax), 0.0), axis=-1, keepdims=True)

    in_group = (lane < N_EXPERTS) & ((lane // EXPERTS_PER_GROUP) == g_idx)
    el = jnp.where(in_group, logits, neg)
    v1 = jnp.max(el, axis=-1, keepdims=True)
    i1 = jnp.min(jnp.where(el == v1, lane, big), axis=-1, keepdims=True)
    el2 = jnp.where(lane == i1, neg, el)
    v2 = jnp.max(el2, axis=-1, keepdims=True)
    i2 = jnp.min(jnp.where(el2 == v2, lane, big), axis=-1, keepdims=True)
    e21 = jnp.exp(v2 - v1)
    den = 1.0 + e21
    w1 = g_w * (1.0 / den)
    w2 = g_w * (e21 / den)

    oh1 = lane == i1
    oh2 = lane == i2
    onehot = (oh1 | oh2).astype(BF16)
    ahead = jnp.dot(ltri_ref[...], onehot, preferred_element_type=F32) + cnt_ref[...]
    r1 = jnp.sum(jnp.where(oh1, ahead, 0.0), axis=-1, keepdims=True)
    r2 = jnp.sum(jnp.where(oh2, ahead, 0.0), axis=-1, keepdims=True)
    cnt_ref[...] += jnp.sum(onehot.astype(F32), axis=0, keepdims=True)

    rec = jnp.where(lane == 0, i1.astype(F32), 0.0)
    rec = jnp.where(lane == 1, i2.astype(F32), rec)
    rec = jnp.where(lane == 2, r1, rec)
    rec = jnp.where(lane == 3, r2, rec)
    rec = jnp.where(lane == 4, w1, rec)
    rec = jnp.where(lane == 5, w2, rec)
    return rec


def _store_token_major(ref, val):
    tm = val.shape[0]
    n = val.shape[1] // LANES
    for c in range(n):
        ref[pl.ds(c, tm, stride=n), :] = val[:, c * LANES:(c + 1) * LANES]


def _load_token_major(ref, tm, n, c):
    return ref[pl.ds(c, tm, stride=n), :]


def _mix_out_epilogue(a, x_ref, wo_ref, gf_ref, wr_ref, br_ref, x1_ref, h3_ref, route_ref, cnt_out_ref,
                      ltri_ref, cnt_ref):
    tm = a.shape[0]

    @pl.when(pl.program_id(0) == 0)
    def _():
        r = lax.broadcasted_iota(jnp.int32, (tm, tm), 0)
        c = lax.broadcasted_iota(jnp.int32, (tm, tm), 1)
        ltri_ref[...] = (c < r).astype(BF16)
        cnt_ref[...] = jnp.zeros_like(cnt_ref)

    x1 = x_ref[...] + jnp.dot(a, wo_ref[...], preferred_element_type=F32)
    x1_ref[...] = x1
    h2 = _rms(x1, gf_ref[...])
    _store_token_major(h3_ref, h2)
    route_ref[...] = _route_tile(h2, wr_ref, br_ref, ltri_ref, cnt_ref)
    cnt_out_ref[...] = cnt_ref[...]


def _attn_out_kernel(a_ref, x_ref, wo_ref, gf_ref, wr_ref, br_ref,
                     x1_ref, h3_ref, route_ref, cnt_out_ref, ltri_ref, cnt_ref):
    _mix_out_epilogue(a_ref[...], x_ref, wo_ref, gf_ref, wr_ref, br_ref,
                      x1_ref, h3_ref, route_ref, cnt_out_ref, ltri_ref, cnt_ref)


def _sg_out_kernel(u_ref, vn_ref, ws_ref, bs_ref, x_ref, wo_ref, gf_ref, wr_ref, br_ref,
                   x1_ref, h3_ref, route_ref, cnt_out_ref, ltri_ref, cnt_ref, a_ref):
    tm, d = u_ref.shape
    r = lax.broadcasted_iota(jnp.int32, (LANES, LANES), 0)
    c = lax.broadcasted_iota(jnp.int32, (LANES, LANES), 1)
    keep = c <= r
    for g in range(d // LANES):
        sl = slice(g * LANES, (g + 1) * LANES)
        wc = jnp.where(keep, ws_ref[g], 0.0).astype(BF16)
        bias = bs_ref[:, g:g + 1]
        for ch in range(tm // LANES):
            rs = slice(ch * LANES, (ch + 1) * LANES)
            mixed = jnp.dot(wc, vn_ref[rs, sl], preferred_element_type=F32) + bias
            a_ref[rs, sl] = (u_ref[rs, sl].astype(F32) * mixed).astype(BF16)
    _mix_out_epilogue(a_ref[...], x_ref, wo_ref, gf_ref, wr_ref, br_ref,
                      x1_ref, h3_ref, route_ref, cnt_out_ref, ltri_ref, cnt_ref)


def _mix_out(x, w_out, g_ffn, w_router, b_router, *, attn_out=None, sg=None):
    t, d = x.shape
    tm = min(TM_ROW, t)
    n_row = d // LANES
    row_spec = pl.BlockSpec((tm, d), lambda i: (i, 0))
    const = lambda shape: pl.BlockSpec(shape, lambda i: (0,) * len(shape))
    common_in = [row_spec, const((d, d)), const((1, d)), const(w_router.shape), const((1, LANES))]
    common_args = (x, w_out, g_ffn, w_router, b_router)
    out_specs = [
        row_spec,
        pl.BlockSpec((tm * n_row, LANES), lambda i: (i, 0)),
        pl.BlockSpec((tm, LANES), lambda i: (i, 0)),
        const((1, LANES)),
    ]
    out_shape = [
        jax.ShapeDtypeStruct((t, d), F32),
        jax.ShapeDtypeStruct((t * n_row, LANES), F32),
        jax.ShapeDtypeStruct((t, LANES), F32),
        jax.ShapeDtypeStruct((1, LANES), F32),
    ]
    scratch = [pltpu.VMEM((tm, tm), BF16), pltpu.VMEM((1, LANES), F32)]
    if attn_out is not None:
        return pl.pallas_call(
            _attn_out_kernel,
            grid=(t // tm,),
            in_specs=[row_spec] + common_in,
            out_specs=out_specs, out_shape=out_shape, scratch_shapes=scratch,
            compiler_params=_cparams("arbitrary"),
            name="attn_out_router",
        )(attn_out, *common_args)
    z, w_s, b_s_t = sg
    return pl.pallas_call(
        _sg_out_kernel,
        grid=(t // tm,),
        in_specs=[
            pl.BlockSpec((tm, d), lambda i: (i, 0)),
            pl.BlockSpec((tm, d), lambda i: (i, 1)),
            const(w_s.shape),
            const(b_s_t.shape),
        ] + common_in,
        out_specs=out_specs, out_shape=out_shape,
        scratch_shapes=scratch + [pltpu.VMEM((tm, d), BF16)],
        compiler_params=_cparams("arbitrary"),
        name="sg_out_router",
    )(z, z, w_s, b_s_t, *common_args)


def _row_slice(ref, row):
    return ref.at[pl.ds(pl.multiple_of(row * ROW_TILES, ROW_TILES), ROW_TILES)]


def _dispatch_kernel(pos_ref, pad_start_ref, pad_n_ref, n_used_ref, h3_ref, xs_ref, zero_ref, sem, pad_sem):
    td = pos_ref.shape[2] // 2
    base = pl.program_id(0) * td

    @pl.when(pl.program_id(0) == 0)
    def _():
        zero_ref[...] = jnp.zeros_like(zero_ref)
        tile_rows = zero_ref.shape[0]
        n_tiles = xs_ref.shape[0] // tile_rows

        def pad_copy(e, r):
            return pltpu.make_async_copy(zero_ref.at[pl.ds(0, ROW_TILES)],
                                         _row_slice(xs_ref, pad_start_ref[e] + r), pad_sem)

        def tile_copy(i):
            return pltpu.make_async_copy(
                zero_ref, xs_ref.at[pl.ds(pl.multiple_of(i * tile_rows, tile_rows), tile_rows)], pad_sem)

        def start_all(e, carry):
            lax.fori_loop(0, pad_n_ref[e], lambda r, c: (pad_copy(e, r).start(), c)[1], 0)
            return carry

        def wait_all(e, carry):
            lax.fori_loop(0, pad_n_ref[e], lambda r, c: (pad_copy(e, r).wait(), c)[1], 0)
            return carry

        lax.fori_loop(0, pad_n_ref.shape[0], start_all, 0)
        lax.fori_loop(n_used_ref[0], n_tiles, lambda i, c: (tile_copy(i).start(), c)[1], 0)
        lax.fori_loop(0, pad_n_ref.shape[0], wait_all, 0)
        lax.fori_loop(n_used_ref[0], n_tiles, lambda i, c: (tile_copy(i).wait(), c)[1], 0)

    def issue(t, carry):
        src = _row_slice(h3_ref, base + t)
        for k in range(2):
            pltpu.make_async_copy(src, _row_slice(xs_ref, pos_ref[0, 0, 2 * t + k]), sem).start()
        return carry

    lax.fori_loop(0, td, issue, 0)
    n = td * ROW_TILES
    for _ in range(2):
        pltpu.make_async_copy(h3_ref.at[pl.ds(0, n)], xs_ref.at[pl.ds(0, n)], sem).wait()


def _dispatch(h3, pos, pad_start, pad_n, n_used, n_sorted_rows):
    t = pos.shape[0]
    td = min(TD_MOVE, t)
    pos3 = pos.reshape(t // td, 1, 2 * td)
    smem = pl.BlockSpec(memory_space=pltpu.SMEM)
    return pl.pallas_call(
        _dispatch_kernel,
        grid=(t // td,),
        in_specs=[
            pl.BlockSpec((1, 1, 2 * td), lambda i: (i, 0, 0), memory_space=pltpu.SMEM),
            smem, smem, smem,
            pl.BlockSpec(memory_space=pl.ANY),
        ],
        out_specs=pl.BlockSpec(memory_space=pl.ANY),
        out_shape=jax.ShapeDtypeStruct((n_sorted_rows * ROW_TILES, LANES), F32),
        scratch_shapes=[pltpu.VMEM((TM_EXPERT * ROW_TILES, LANES), F32), pltpu.SemaphoreType.DMA(()),
                        pltpu.SemaphoreType.DMA(())],
        compiler_params=pltpu.CompilerParams(dimension_semantics=("arbitrary",)),
        name="moe_dispatch",
    )(pos3, pad_start, pad_n, n_used, h3)


def _combine_kernel(pos_ref, ys_ref, y2_ref, sem):
    td = pos_ref.shape[2] // 2
    base = pl.program_id(0) * td

    def issue(t, carry):
        for k in range(2):
            src = _row_slice(ys_ref, pos_ref[0, 0, 2 * t + k])
            pltpu.make_async_copy(src, _row_slice(y2_ref.at[k], base + t), sem).start()
        return carry

    lax.fori_loop(0, td, issue, 0)
    n = td * ROW_TILES
    pltpu.make_async_copy(ys_ref.at[pl.ds(0, n)], y2_ref.at[0, pl.ds(0, n)], sem).wait()
    pltpu.make_async_copy(ys_ref.at[pl.ds(0, n)], y2_ref.at[1, pl.ds(0, n)], sem).wait()


def _combine(ys, pos):
    t = pos.shape[0]
    td = min(TD_MOVE, t)
    pos3 = pos.reshape(t // td, 1, 2 * td)
    return pl.pallas_call(
        _combine_kernel,
        grid=(t // td,),
        in_specs=[
            pl.BlockSpec((1, 1, 2 * td), lambda i: (i, 0, 0), memory_space=pltpu.SMEM),
            pl.BlockSpec(memory_space=pl.ANY),
        ],
        out_specs=pl.BlockSpec(memory_space=pl.ANY),
        out_shape=jax.ShapeDtypeStruct((2, t * ROW_TILES, LANES), F32),
        scratch_shapes=[pltpu.SemaphoreType.DMA(())],
        compiler_params=pltpu.CompilerParams(dimension_semantics=("arbitrary",), has_side_effects=True),
        name="moe_combine",
    )(pos3, ys)


def _expert_kernel(tile_expert_ref, n_used_ref, xs_ref, wgu_ref, wd_ref, ys_ref, a_ref):
    i = pl.program_id(0)
    tm, d = a_ref.shape
    n_row = d // LANES
    f = wd_ref.shape[0]

    @pl.when(i < n_used_ref[0])
    def _():
        for c in range(n_row):
            a_ref[:, c * LANES:(c + 1) * LANES] = _load_token_major(xs_ref, tm, n_row, c).astype(BF16)
        gu = jnp.dot(a_ref[...], wgu_ref[...], preferred_element_type=F32)
        hidden = (jax.nn.silu(gu[:, :f]) * gu[:, f:]).astype(BF16)
        _store_token_major(ys_ref, jnp.dot(hidden, wd_ref[...], preferred_element_type=F32))

    @pl.when(i >= n_used_ref[0])
    def _():
        ys_ref[...] = jnp.zeros_like(ys_ref)


def _experts(xs, w_gu, w_down, tile_expert, n_used):
    e, d, f2 = w_gu.shape
    n_rows = xs.shape[0] // ROW_TILES
    tm = TM_EXPERT
    n_tiles = n_rows // tm
    return pl.pallas_call(
        _expert_kernel,
        grid_spec=pltpu.PrefetchScalarGridSpec(
            num_scalar_prefetch=2,
            grid=(n_tiles,),
            in_specs=[
                pl.BlockSpec((tm * ROW_TILES, LANES), lambda i, te, nu: (jnp.minimum(i, nu[0] - 1), 0)),
                pl.BlockSpec((None, d, f2), lambda i, te, nu: (te[i], 0, 0)),
                pl.BlockSpec((None, f2 // 2, d), lambda i, te, nu: (te[i], 0, 0)),
            ],
            out_specs=pl.BlockSpec((tm * ROW_TILES, LANES), lambda i, te, nu: (i, 0)),
            scratch_shapes=[pltpu.VMEM((tm, d), BF16)],
        ),
        out_shape=jax.ShapeDtypeStruct(xs.shape, F32),
        compiler_params=_cparams("arbitrary"),
        name="moe_experts",
    )(tile_expert, n_used, xs, w_gu, w_down)


def _ple_kernel(x1_ref, y2_ref, route_ref, p_ref, gin_ref, wg_ref, wp_ref, gout_ref, o_ref, x2_ref):
    tm, d = x1_ref.shape
    n_row = d // LANES
    w0 = route_ref[:, 4:5]
    w1 = route_ref[:, 5:6]
    for c in range(n_row):
        sl = slice(c * LANES, (c + 1) * LANES)
        y0 = _load_token_major(y2_ref.at[0], tm, n_row, c)
        y1 = _load_token_major(y2_ref.at[1], tm, n_row, c)
        x2_ref[:, sl] = x1_ref[:, sl] + w0 * y0 + w1 * y1
    x2 = x2_ref[...]
    gate = jax.nn.sigmoid(jnp.dot(_rms(x2, gin_ref[...]).astype(BF16), wg_ref[...],
                                  preferred_element_type=F32))
    e = jnp.dot(p_ref[...].astype(BF16), wp_ref[...], preferred_element_type=F32) * gate
    o_ref[...] = x2 + _rms(e, gout_ref[...])


def _ple(x1, y2, route, p, g_in, w_gate, w_proj, g_out):
    t, d = x1.shape
    tm = min(TM_ROW, t)
    n_row = d // LANES
    pd = p.shape[1]
    row_spec = pl.BlockSpec((tm, d), lambda i: (i, 0))
    const = lambda shape: pl.BlockSpec(shape, lambda i: (0,) * len(shape))
    return pl.pallas_call(
        _ple_kernel,
        grid=(t // tm,),
        in_specs=[
            row_spec,
            pl.BlockSpec((2, tm * n_row, LANES), lambda i: (0, i, 0)),
            pl.BlockSpec((tm, LANES), lambda i: (i, 0)),
            pl.BlockSpec((tm, pd), lambda i: (i, 0)),
            const((1, d)), const((d, d)), const((pd, d)), const((1, d)),
        ],
        out_specs=row_spec,
        out_shape=jax.ShapeDtypeStruct((t, d), F32),
        scratch_shapes=[pltpu.VMEM((tm, d), F32)],
        compiler_params=_cparams("parallel"),
        name="moe_combine_ple",
    )(x1, y2, route, p, g_in, w_gate, w_proj, g_out)


def _router_weights(w_group, b_group, w_expert, b_expert):
    d = w_group.shape[0]
    w = jnp.zeros((d, LANES), F32)
    w = w.at[:, :N_EXPERTS].set(w_expert).at[:, ROUTER_GROUP_LANE0:ROUTER_GROUP_LANE0 + N_GROUPS].set(w_group)
    b = jnp.zeros((1, LANES), F32)
    b = b.at[0, :N_EXPERTS].set(b_expert).at[0, ROUTER_GROUP_LANE0:ROUTER_GROUP_LANE0 + N_GROUPS].set(b_group)
    w_hi = w.astype(BF16)
    w_lo = (w - w_hi.astype(F32)).astype(BF16)
    return jnp.concatenate([w_hi, w_hi, w_lo], axis=0), b


def _sorted_layout(route, counts, t):
    tm = TM_EXPERT
    n_tiles = (2 * t + N_EXPERTS * (tm - 1)) // tm + 1
    eid = route[:, 0:2].astype(jnp.int32)
    rank = route[:, 2:4].astype(jnp.int32)
    cnt = counts[0, :N_EXPERTS].astype(jnp.int32)
    padded = ((cnt + tm - 1) // tm) * tm
    ends = jnp.cumsum(padded)
    offs = ends - padded
    pos = offs[eid] + rank
    tile_start = jnp.arange(n_tiles, dtype=jnp.int32) * tm
    n_used = (ends[-1] // tm).astype(jnp.int32)
    te = jnp.searchsorted(ends, jnp.minimum(tile_start, ends[-1] - tm), side="right").astype(jnp.int32)
    return dict(pos=pos, tile_expert=te, n_used=n_used.reshape(1), pad_start=offs + cnt,
                pad_n=padded - cnt, n_sorted=n_tiles * tm)


def _moe_and_ple(x, mix_out_kwargs, w_out, g_ffn, router, w_gu, w_down, p_i, ple):
    t = x.shape[0]
    x1, h3, route, counts = _mix_out(x, w_out, g_ffn, *router, **mix_out_kwargs)
    lay = _sorted_layout(route, counts, t)
    xs = _dispatch(h3, lay["pos"], lay["pad_start"], lay["pad_n"], lay["n_used"], lay["n_sorted"])
    ys = _experts(xs, w_gu, w_down, lay["tile_expert"], lay["n_used"])
    y2 = _combine(ys, lay["pos"])
    return _ple(x1, y2, route, p_i, *ple)


def kernel(x, p, norm_mix, norm_ffn, sb_w_in, sb_q_norm, sb_k_norm, sb_w_out, sg_w_in, sg_v_norm, sg_w_s, sg_b_s, sg_w_out, moe_w_group, moe_b_group, moe_w_expert, moe_b_expert, moe_w_gate, moe_w_up, moe_w_down, ple_norm_in, ple_w_gate, ple_w_proj, ple_norm_out):
    b, s, d = x.shape
    depth = norm_mix.shape[0]
    heads = d // LANES
    assert d == ROW_TILES * LANES
    t = b * s
    xt = x.reshape(t, d)
    row = lambda v: v.reshape(1, -1)
    for i in range(depth):
        j = i // 2
        router = _router_weights(moe_w_group[i], moe_b_group[i], moe_w_expert[i], moe_b_expert[i])
        w_gu = jnp.concatenate([moe_w_gate[i], moe_w_up[i]], axis=-1).astype(BF16)
        w_down = moe_w_down[i].astype(BF16)
        ple = (row(ple_norm_in[i]), ple_w_gate[i].astype(BF16), ple_w_proj[i].astype(BF16), row(ple_norm_out[i]))
        if i % 2 == 0:
            q_gain = sb_q_norm[j] * (LANES ** -0.5 * LOG2E)
            colgain = jnp.concatenate([jnp.tile(q_gain, heads), jnp.tile(sb_k_norm[j], heads),
                                       jnp.ones((d,), F32)]).reshape(1, 3 * d)
            qkv = _qkv_proj(xt, row(norm_mix[i]), sb_w_in[j].astype(BF16), colgain)
            o = _stick_breaking(qkv.reshape(b, s, 3 * d), heads).reshape(t, d)
            mix = dict(attn_out=o)
            w_out = sb_w_out[j]
        else:
            z = _sg_in_proj(xt, row(norm_mix[i]), sg_w_in[j].astype(BF16), row(sg_v_norm[j]))
            mix = dict(sg=(z, sg_w_s[j], sg_b_s[j].T))
            w_out = sg_w_out[j]
        xt = _moe_and_ple(xt, mix, w_out.astype(BF16), row(norm_ffn[i]), router, w_gu, w_down,
                          p[i].reshape(t, -1), ple)
    return xt.reshape(b, s, d)
```

```python
import functools

import jax
import jax.numpy as jnp
from jax import lax
from jax.experimental import pallas as pl
from jax.experimental.pallas import tpu as pltpu

F32 = jnp.float32
BF16 = jnp.bfloat16

LANES = 128
ROW_TILES = 16
VMEM_LIMIT_BYTES = 56 * 1024 * 1024
EPS = 1e-6
LOG2E = 1.4426950408889634
INV_LN2 = LOG2E

N_GROUPS = 4
EXPERTS_PER_GROUP = 8
N_EXPERTS = N_GROUPS * EXPERTS_PER_GROUP
ROUTER_GROUP_LANE0 = N_EXPERTS

TM_PROJ = 512
TM_ROW = 256
TQ = 256
HEADS_PER_STEP = 2
TM_EXPERT = 256


def _cparams(*sem):
    return pltpu.CompilerParams(dimension_semantics=sem, vmem_limit_bytes=VMEM_LIMIT_BYTES)


def _rms(x, g):
    ms = jnp.mean(x * x, axis=-1, keepdims=True)
    return x * lax.rsqrt(ms + EPS) * g


def _qkv_kernel(x_ref, g_ref, w_ref, cg_ref, o_ref, xn_ref, *, n_norm_tiles):
    j = pl.program_id(1)

    @pl.when(j == 0)
    def _():
        xn_ref[...] = _rms(x_ref[...], g_ref[...]).astype(BF16)

    acc = jnp.dot(xn_ref[...], w_ref[...], preferred_element_type=F32)

    @pl.when(j < n_norm_tiles)
    def _():
        for h in range(acc.shape[1] // LANES):
            sl = slice(h * LANES, (h + 1) * LANES)
            a = acc[:, sl]
            ms = jnp.mean(a * a, axis=-1, keepdims=True)
            o_ref[:, sl] = (a * lax.rsqrt(ms + EPS) * cg_ref[:, sl]).astype(BF16)

    @pl.when(j >= n_norm_tiles)
    def _():
        o_ref[...] = acc.astype(BF16)


def _qkv_proj(x, g, w, colgain, *, tn=1024):
    t, d = x.shape
    n = w.shape[1]
    tm = min(TM_PROJ, t)
    tn = min(tn, d)
    return pl.pallas_call(
        functools.partial(_qkv_kernel, n_norm_tiles=2 * d // tn),
        grid=(t // tm, n // tn),
        in_specs=[
            pl.BlockSpec((tm, d), lambda i, j: (i, 0)),
            pl.BlockSpec((1, d), lambda i, j: (0, 0)),
            pl.BlockSpec((d, tn), lambda i, j: (0, j)),
            pl.BlockSpec((1, tn), lambda i, j: (0, j)),
        ],
        out_specs=pl.BlockSpec((tm, tn), lambda i, j: (i, j)),
        out_shape=jax.ShapeDtypeStruct((t, n), BF16),
        scratch_shapes=[pltpu.VMEM((tm, d), BF16)],
        compiler_params=_cparams("parallel", "arbitrary"),
        name="qkv_proj",
    )(x, g, w, colgain)


def _sg_in_kernel(x_ref, g_ref, w_ref, vg_ref, o_ref, xn_ref):
    j = pl.program_id(1)

    @pl.when(j == 0)
    def _():
        xn_ref[...] = _rms(x_ref[...], g_ref[...]).astype(BF16)

    z = jax.nn.gelu(jnp.dot(xn_ref[...], w_ref[...], preferred_element_type=F32))

    @pl.when(j == 0)
    def _():
        o_ref[...] = z.astype(BF16)

    @pl.when(j == 1)
    def _():
        o_ref[...] = _rms(z, vg_ref[...]).astype(BF16)


def _sg_in_proj(x, g, w, v_gain):
    t, d = x.shape
    tm = min(TM_PROJ, t)
    return pl.pallas_call(
        _sg_in_kernel,
        grid=(t // tm, 2),
        in_specs=[
            pl.BlockSpec((tm, d), lambda i, j: (i, 0)),
            pl.BlockSpec((1, d), lambda i, j: (0, 0)),
            pl.BlockSpec((d, d), lambda i, j: (0, j)),
            pl.BlockSpec((1, d), lambda i, j: (0, 0)),
        ],
        out_specs=pl.BlockSpec((tm, d), lambda i, j: (i, j)),
        out_shape=jax.ShapeDtypeStruct((t, 2 * d), BF16),
        scratch_shapes=[pltpu.VMEM((tm, d), BF16)],
        compiler_params=_cparams("parallel", "arbitrary"),
        name="sg_in_proj",
    )(x, g, w, v_gain)


def _attn_kernel(q_ref, k_ref, v_ref, uu_ref, o_ref):
    qi = pl.program_id(2)
    tq = q_ref.shape[0]
    tk = uu_ref.shape[1]
    n_heads = q_ref.shape[1] // LANES
    head = lambda ref_or_val, h: ref_or_val[:, h * LANES:(h + 1) * LANES]
    qs = [head(q_ref, h) for h in range(n_heads)]

    def tile(kj, state, causal):
        rows_k = pl.ds(pl.multiple_of(kj * tk, tk), tk)
        k_all = k_ref[rows_k, :]
        v_all = v_ref[rows_k, :]
        out = []
        for h in range(n_heads):
            carry, acc = state[h]
            z = lax.dot_general(qs[h], head(k_all, h), (((1,), (1,)), ((), ())), preferred_element_type=F32)
            sp = jnp.maximum(z, 0.0) + jnp.log(1.0 + jnp.exp2(-jnp.abs(z))) * INV_LN2
            if causal is not None:
                sp = jnp.where(causal, sp, 0.0)
            hi = sp.astype(BF16)
            lo = (sp - hi.astype(F32)).astype(BF16)
            c = jnp.dot(jnp.concatenate([hi, lo], axis=1), uu_ref[...], preferred_element_type=F32)
            w = jnp.exp2(z - c - carry)
            if causal is not None:
                w = jnp.where(causal, w, 0.0)
            acc = acc + jnp.dot(w.astype(BF16), head(v_all, h), preferred_element_type=F32)
            out.append((carry + c[:, 0:1], acc))
        return tuple(out)

    rows = lax.broadcasted_iota(jnp.int32, (tq, tk), 0)
    cols = lax.broadcasted_iota(jnp.int32, (tq, tk), 1)
    state0 = tuple((jnp.zeros((tq, 1), F32), jnp.zeros((tq, LANES), F32)) for _ in range(n_heads))
    state = tile(qi, state0, cols < rows)
    state = lax.fori_loop(0, qi, lambda it, st: tile(qi - 1 - it, st, None), state)
    for h in range(n_heads):
        o_ref[:, h * LANES:(h + 1) * LANES] = state[h][1].astype(o_ref.dtype)


def _stick_breaking(qkv, heads):
    b, s, d3 = qkv.shape
    d = d3 // 3
    tq = min(TQ, s)
    tk = tq
    hw = HEADS_PER_STEP * LANES
    groups = heads // HEADS_PER_STEP
    j = lax.broadcasted_iota(jnp.int32, (tk, tk), 0)
    c = lax.broadcasted_iota(jnp.int32, (tk, tk), 1)
    u = (j >= c).astype(BF16)
    uu = jnp.concatenate([u, u], axis=0)
    return pl.pallas_call(
        _attn_kernel,
        grid=(b, groups, s // tq),
        in_specs=[
            pl.BlockSpec((None, tq, hw), lambda bi, h, i: (bi, i, h)),
            pl.BlockSpec((None, s, hw), lambda bi, h, i: (bi, 0, groups + h)),
            pl.BlockSpec((None, s, hw), lambda bi, h, i: (bi, 0, 2 * groups + h)),
            pl.BlockSpec((2 * tk, tk), lambda bi, h, i: (0, 0)),
        ],
        out_specs=pl.BlockSpec((None, tq, hw), lambda bi, h, i: (bi, i, h)),
        out_shape=jax.ShapeDtypeStruct((b, s, d), BF16),
        compiler_params=_cparams("parallel", "parallel", "arbitrary"),
        name="stick_breaking_attention",
    )(qkv, qkv, qkv, uu)


def _split_hi_lo(x):
    hi = x.astype(BF16)
    return hi, (x - hi.astype(F32)).astype(BF16)


def _route_tile(h2, wr_ref, br_ref, ltri_ref, cnt_ref):
    tm = h2.shape[0]
    hi, lo = _split_hi_lo(h2)
    lhs = jnp.concatenate([hi, lo, hi], axis=1)
    logits = jnp.dot(lhs, wr_ref[...], preferred_element_type=F32) + br_ref[...]

    lane = lax.broadcasted_iota(jnp.int32, (tm, LANES), 1)
    neg = jnp.float32(-jnp.inf)
    big = jnp.int32(LANES)

    is_group = (lane >= ROUTER_GROUP_LANE0) & (lane < ROUTER_GROUP_LANE0 + N_GROUPS)
    gl = jnp.where(is_group, logits, neg)
    gmax = jnp.max(gl, axis=-1, keepdims=True)
    g_idx = jnp.min(jnp.where(gl == gmax, lane - ROUTER_GROUP_LANE0, big), axis=-1, keepdims=True)
    g_w = 1.0 / jnp.sum(jnp.where(is_group, jnp.exp(gl - gmax), 0.0), axis=-1, keepdims=True)

    in_group = (lane < N_EXPERTS) & ((lane // EXPERTS_PER_GROUP) == g_idx)
    el = jnp.where(in_group, logits, neg)
    v1 = jnp.max(el, axis=-1, keepdims=True)
    i1 = jnp.min(jnp.where(el == v1, lane, big), axis=-1, keepdims=True)
    el2 = jnp.where(lane == i1, neg, el)
    v2 = jnp.max(el2, axis=-1, keepdims=True)
    i2 = jnp.min(jnp.where(el2 == v2, lane, big), axis=-1, keepdims=True)
    e21 = jnp.exp(v2 - v1)
    den = 1.0 + e21
    w1 = g_w * (1.0 / den)
    w2 = g_w * (e21 / den)

    oh1 = lane == i1
    oh2 = lane == i2
    onehot = (oh1 | oh2).astype(BF16)
    ahead = jnp.dot(ltri_ref[...], onehot, preferred_element_type=F32) + cnt_ref[...]
    r1 = jnp.sum(jnp.where(oh1, ahead, 0.0), axis=-1, keepdims=True)
    r2 = jnp.sum(jnp.where(oh2, ahead, 0.0), axis=-1, keepdims=True)
    cnt_ref[...] += jnp.sum(onehot.astype(F32), axis=0, keepdims=True)

    rec = jnp.where(lane == 0, i1.astype(F32), 0.0)
    rec = jnp.where(lane == 1, i2.astype(F32), rec)
    rec = jnp.where(lane == 2, r1, rec)
    rec = jnp.where(lane == 3, r2, rec)
    rec = jnp.where(lane == 4, w1, rec)
    rec = jnp.where(lane == 5, w2, rec)
    return rec


def _store_token_major(ref, val):
    tm = val.shape[0]
    n = val.shape[1] // LANES
    for c in range(n):
        ref[pl.ds(c, tm, stride=n), :] = val[:, c * LANES:(c + 1) * LANES]


def _load_token_major(ref, tm, n, c):
    return ref[pl.ds(c, tm, stride=n), :]


def _mix_out_epilogue(a, x_ref, wo_ref, gf_ref, wr_ref, br_ref, x1_ref, route_ref, cnt_out_ref,
                      ltri_ref, cnt_ref):
    tm = a.shape[0]

    @pl.when(pl.program_id(0) == 0)
    def _():
        r = lax.broadcasted_iota(jnp.int32, (tm, tm), 0)
        c = lax.broadcasted_iota(jnp.int32, (tm, tm), 1)
        ltri_ref[...] = (c < r).astype(BF16)
        cnt_ref[...] = jnp.zeros_like(cnt_ref)

    x1 = x_ref[...] + jnp.dot(a, wo_ref[...], preferred_element_type=F32)
    x1_ref[...] = x1
    route_ref[...] = _route_tile(_rms(x1, gf_ref[...]), wr_ref, br_ref, ltri_ref, cnt_ref)
    cnt_out_ref[...] = cnt_ref[...]


def _attn_out_kernel(a_ref, x_ref, wo_ref, gf_ref, wr_ref, br_ref,
                     x1_ref, route_ref, cnt_out_ref, ltri_ref, cnt_ref):
    _mix_out_epilogue(a_ref[...], x_ref, wo_ref, gf_ref, wr_ref, br_ref,
                      x1_ref, route_ref, cnt_out_ref, ltri_ref, cnt_ref)


def _sg_out_kernel(u_ref, vn_ref, ws_ref, bs_ref, x_ref, wo_ref, gf_ref, wr_ref, br_ref,
                   x1_ref, route_ref, cnt_out_ref, ltri_ref, cnt_ref, a_ref):
    tm, d = u_ref.shape
    r = lax.broadcasted_iota(jnp.int32, (LANES, LANES), 0)
    c = lax.broadcasted_iota(jnp.int32, (LANES, LANES), 1)
    keep = c <= r
    for g in range(d // LANES):
        sl = slice(g * LANES, (g + 1) * LANES)
        wc = jnp.where(keep, ws_ref[g], 0.0).astype(BF16)
        bias = bs_ref[:, g:g + 1]
        for ch in range(tm // LANES):
            rs = slice(ch * LANES, (ch + 1) * LANES)
            mixed = jnp.dot(wc, vn_ref[rs, sl], preferred_element_type=F32) + bias
            a_ref[rs, sl] = (u_ref[rs, sl].astype(F32) * mixed).astype(BF16)
    _mix_out_epilogue(a_ref[...], x_ref, wo_ref, gf_ref, wr_ref, br_ref,
                      x1_ref, route_ref, cnt_out_ref, ltri_ref, cnt_ref)


def _mix_out(x, w_out, g_ffn, w_router, b_router, *, attn_out=None, sg=None):
    t, d = x.shape
    tm = min(TM_ROW, t)
    row_spec = pl.BlockSpec((tm, d), lambda i: (i, 0))
    const = lambda shape: pl.BlockSpec(shape, lambda i: (0,) * len(shape))
    common_in = [row_spec, const((d, d)), const((1, d)), const(w_router.shape), const((1, LANES))]
    common_args = (x, w_out, g_ffn, w_router, b_router)
    out_specs = [row_spec, pl.BlockSpec((tm, LANES), lambda i: (i, 0)), const((1, LANES))]
    out_shape = [
        jax.ShapeDtypeStruct((t, d), F32),
        jax.ShapeDtypeStruct((t, LANES), F32),
        jax.ShapeDtypeStruct((1, LANES), F32),
    ]
    scratch = [pltpu.VMEM((tm, tm), BF16), pltpu.VMEM((1, LANES), F32)]
    if attn_out is not None:
        return pl.pallas_call(
            _attn_out_kernel,
            grid=(t // tm,),
            in_specs=[row_spec] + common_in,
            out_specs=out_specs, out_shape=out_shape, scratch_shapes=scratch,
            compiler_params=_cparams("arbitrary"),
            name="attn_out_router",
        )(attn_out, *common_args)
    z, w_s, b_s_t = sg
    return pl.pallas_call(
        _sg_out_kernel,
        grid=(t // tm,),
        in_specs=[
            pl.BlockSpec((tm, d), lambda i: (i, 0)),
            pl.BlockSpec((tm, d), lambda i: (i, 1)),
            const(w_s.shape),
            const(b_s_t.shape),
        ] + common_in,
        out_specs=out_specs, out_shape=out_shape,
        scratch_shapes=scratch + [pltpu.VMEM((tm, d), BF16)],
        compiler_params=_cparams("arbitrary"),
        name="sg_out_router",
    )(z, z, w_s, b_s_t, *common_args)


def _row_slice(ref, row):
    return ref.at[pl.ds(pl.multiple_of(row * ROW_TILES, ROW_TILES), ROW_TILES)]


def _dispatch_kernel(pos_ref, pad_start_ref, pad_n_ref, n_used_ref, x1_ref, g_ref, xs_ref,
                     f32_ref, stage_ref, zero_ref, sems, pad_sem):
    i = pl.program_id(0)
    n_steps = pl.num_programs(0)
    td = x1_ref.shape[0]
    slot = i % 2

    def wait_slot(s):
        for _ in range(2):
            pltpu.make_async_copy(stage_ref.at[s], xs_ref.at[pl.ds(0, td * ROW_TILES)], sems.at[s]).wait()

    @pl.when(i == 0)
    def _():
        zero_ref[...] = jnp.zeros_like(zero_ref)
        tile_rows = zero_ref.shape[0]
        n_tiles = xs_ref.shape[0] // tile_rows

        def pad_copy(e, r):
            return pltpu.make_async_copy(zero_ref.at[pl.ds(0, ROW_TILES)],
                                         _row_slice(xs_ref, pad_start_ref[e] + r), pad_sem)

        def tile_copy(j):
            return pltpu.make_async_copy(
                zero_ref, xs_ref.at[pl.ds(pl.multiple_of(j * tile_rows, tile_rows), tile_rows)], pad_sem)

        def start_all(e, carry):
            lax.fori_loop(0, pad_n_ref[e], lambda r, c: (pad_copy(e, r).start(), c)[1], 0)
            return carry

        def wait_all(e, carry):
            lax.fori_loop(0, pad_n_ref[e], lambda r, c: (pad_copy(e, r).wait(), c)[1], 0)
            return carry

        lax.fori_loop(0, pad_n_ref.shape[0], start_all, 0)
        lax.fori_loop(n_used_ref[0], n_tiles, lambda j, c: (tile_copy(j).start(), c)[1], 0)
        lax.fori_loop(0, pad_n_ref.shape[0], wait_all, 0)
        lax.fori_loop(n_used_ref[0], n_tiles, lambda j, c: (tile_copy(j).wait(), c)[1], 0)

    @pl.when(i >= 2)
    def _():
        wait_slot(slot)

    _store_token_major(f32_ref, _rms(x1_ref[...], g_ref[...]))
    stage_ref[slot] = f32_ref[...].astype(BF16)

    def issue(t, carry):
        src = stage_ref.at[slot, pl.ds(pl.multiple_of(t * ROW_TILES, ROW_TILES), ROW_TILES)]
        for k in range(2):
            pltpu.make_async_copy(src, _row_slice(xs_ref, pos_ref[0, 0, 2 * t + k]), sems.at[slot]).start()
        return carry

    lax.fori_loop(0, td, issue, 0)

    @pl.when(i == n_steps - 1)
    def _():
        wait_slot(slot)

    @pl.when((i == n_steps - 1) & (i >= 1))
    def _():
        wait_slot(1 - slot)


def _dispatch(x1, g_ffn, pos, pad_start, pad_n, n_used, n_sorted_rows):
    t, d = x1.shape
    td = min(TM_ROW, t)
    pos3 = pos.reshape(t // td, 1, 2 * td)
    smem = pl.BlockSpec(memory_space=pltpu.SMEM)
    return pl.pallas_call(
        _dispatch_kernel,
        grid=(t // td,),
        in_specs=[
            pl.BlockSpec((1, 1, 2 * td), lambda i: (i, 0, 0), memory_space=pltpu.SMEM),
            smem, smem, smem,
            pl.BlockSpec((td, d), lambda i: (i, 0)),
            pl.BlockSpec((1, d), lambda i: (0, 0)),
        ],
        out_specs=pl.BlockSpec(memory_space=pl.ANY),
        out_shape=jax.ShapeDtypeStruct((n_sorted_rows * ROW_TILES, LANES), BF16),
        scratch_shapes=[
            pltpu.VMEM((td * ROW_TILES, LANES), F32),
            pltpu.VMEM((2, td * ROW_TILES, LANES), BF16),
            pltpu.VMEM((TM_EXPERT * ROW_TILES, LANES), BF16),
            pltpu.SemaphoreType.DMA((2,)),
            pltpu.SemaphoreType.DMA(()),
        ],
        compiler_params=_cparams("arbitrary"),
        name="moe_dispatch",
    )(pos3, pad_start, pad_n, n_used, x1, g_ffn)


def _expert_kernel(tile_expert_ref, n_used_ref, xs_ref, wgu_ref, wd_ref, ys_ref, f32_ref, a_ref):
    i = pl.program_id(0)
    tm, d = a_ref.shape
    n_row = d // LANES
    f = wd_ref.shape[0]

    @pl.when(i < n_used_ref[0])
    def _():
        f32_ref[...] = xs_ref[...].astype(F32)
        for c in range(n_row):
            a_ref[:, c * LANES:(c + 1) * LANES] = _load_token_major(f32_ref, tm, n_row, c).astype(BF16)
        gu = jnp.dot(a_ref[...], wgu_ref[...], preferred_element_type=F32)
        hidden = (jax.nn.silu(gu[:, :f]) * gu[:, f:]).astype(BF16)
        _store_token_major(f32_ref, jnp.dot(hidden, wd_ref[...], preferred_element_type=F32))
        ys_ref[...] = f32_ref[...].astype(BF16)

    @pl.when(i >= n_used_ref[0])
    def _():
        ys_ref[...] = jnp.zeros_like(ys_ref)


def _experts(xs, w_gu, w_down, tile_expert, n_used):
    e, d, f2 = w_gu.shape
    n_rows = xs.shape[0] // ROW_TILES
    tm = TM_EXPERT
    n_tiles = n_rows // tm
    return pl.pallas_call(
        _expert_kernel,
        grid_spec=pltpu.PrefetchScalarGridSpec(
            num_scalar_prefetch=2,
            grid=(n_tiles,),
            in_specs=[
                pl.BlockSpec((tm * ROW_TILES, LANES), lambda i, te, nu: (jnp.minimum(i, nu[0] - 1), 0)),
                pl.BlockSpec((None, d, f2), lambda i, te, nu: (te[i], 0, 0)),
                pl.BlockSpec((None, f2 // 2, d), lambda i, te, nu: (te[i], 0, 0)),
            ],
            out_specs=pl.BlockSpec((tm * ROW_TILES, LANES), lambda i, te, nu: (i, 0)),
            scratch_shapes=[pltpu.VMEM((tm * ROW_TILES, LANES), F32), pltpu.VMEM((tm, d), BF16)],
        ),
        out_shape=jax.ShapeDtypeStruct(xs.shape, BF16),
        compiler_params=_cparams("arbitrary"),
        name="moe_experts",
    )(tile_expert, n_used, xs, w_gu, w_down)


def _ple_kernel(pos_ref, pos_next_ref, x1_ref, route_ref, p_ref, gin_ref, wg_ref, wp_ref, gout_ref, ys_ref,
                o_ref, ybuf_ref, f32_ref, x2_ref, sems):
    i = pl.program_id(0)
    n_steps = pl.num_programs(0)
    tm, d = x1_ref.shape
    n_row = d // LANES
    slot = i % 2

    def gather(p_ref_, s):
        def issue(t, carry):
            for k in range(2):
                dst = ybuf_ref.at[s, k, pl.ds(pl.multiple_of(t * ROW_TILES, ROW_TILES), ROW_TILES)]
                pltpu.make_async_copy(_row_slice(ys_ref, p_ref_[0, 0, 2 * t + k]), dst, sems.at[s]).start()
            return carry
        lax.fori_loop(0, tm, issue, 0)

    @pl.when(i == 0)
    def _():
        gather(pos_ref, 0)

    @pl.when(i + 1 < n_steps)
    def _():
        gather(pos_next_ref, 1 - slot)

    for k in range(2):
        pltpu.make_async_copy(ys_ref.at[pl.ds(0, tm * ROW_TILES)], ybuf_ref.at[slot, k], sems.at[slot]).wait()

    w = (route_ref[:, 4:5], route_ref[:, 5:6])
    for k in range(2):
        f32_ref[k] = ybuf_ref[slot, k].astype(F32)
    for c in range(n_row):
        sl = slice(c * LANES, (c + 1) * LANES)
        y0 = _load_token_major(f32_ref.at[0], tm, n_row, c)
        y1 = _load_token_major(f32_ref.at[1], tm, n_row, c)
        x2_ref[:, sl] = x1_ref[:, sl] + w[0] * y0 + w[1] * y1
    x2 = x2_ref[...]
    gate = jax.nn.sigmoid(jnp.dot(_rms(x2, gin_ref[...]).astype(BF16), wg_ref[...],
                                  preferred_element_type=F32))
    e = jnp.dot(p_ref[...].astype(BF16), wp_ref[...], preferred_element_type=F32) * gate
    o_ref[...] = x2 + _rms(e, gout_ref[...])


def _ple(x1, ys, pos, route, p, g_in, w_gate, w_proj, g_out):
    t, d = x1.shape
    tm = min(TM_ROW, t)
    n_steps = t // tm
    pd = p.shape[1]
    pos3 = pos.reshape(n_steps, 1, 2 * tm)
    row_spec = pl.BlockSpec((tm, d), lambda i: (i, 0))
    const = lambda shape: pl.BlockSpec(shape, lambda i: (0,) * len(shape))
    return pl.pallas_call(
        _ple_kernel,
        grid=(n_steps,),
        in_specs=[
            pl.BlockSpec((1, 1, 2 * tm), lambda i: (i, 0, 0), memory_space=pltpu.SMEM),
            pl.BlockSpec((1, 1, 2 * tm), lambda i: (jnp.minimum(i + 1, n_steps - 1), 0, 0),
                         memory_space=pltpu.SMEM),
            row_spec,
            pl.BlockSpec((tm, LANES), lambda i: (i, 0)),
            pl.BlockSpec((tm, pd), lambda i: (i, 0)),
            const((1, d)), const((d, d)), const((pd, d)), const((1, d)),
            pl.BlockSpec(memory_space=pl.ANY),
        ],
        out_specs=row_spec,
        out_shape=jax.ShapeDtypeStruct((t, d), F32),
        scratch_shapes=[
            pltpu.VMEM((2, 2, tm * ROW_TILES, LANES), BF16),
            pltpu.VMEM((2, tm * ROW_TILES, LANES), F32),
            pltpu.VMEM((tm, d), F32),
            pltpu.SemaphoreType.DMA((2,)),
        ],
        compiler_params=_cparams("arbitrary"),
        name="moe_combine_ple",
    )(pos3, pos3, x1, route, p, g_in, w_gate, w_proj, g_out, ys)


def _router_weights(w_group, b_group, w_expert, b_expert):
    d = w_group.shape[0]
    w = jnp.zeros((d, LANES), F32)
    w = w.at[:, :N_EXPERTS].set(w_expert).at[:, ROUTER_GROUP_LANE0:ROUTER_GROUP_LANE0 + N_GROUPS].set(w_group)
    b = jnp.zeros((1, LANES), F32)
    b = b.at[0, :N_EXPERTS].set(b_expert).at[0, ROUTER_GROUP_LANE0:ROUTER_GROUP_LANE0 + N_GROUPS].set(b_group)
    w_hi = w.astype(BF16)
    w_lo = (w - w_hi.astype(F32)).astype(BF16)
    return jnp.concatenate([w_hi, w_hi, w_lo], axis=0), b


def _sorted_layout(route, counts, t):
    tm = TM_EXPERT
    n_tiles = (2 * t + N_EXPERTS * (tm - 1)) // tm + 1
    eid = route[:, 0:2].astype(jnp.int32)
    rank = route[:, 2:4].astype(jnp.int32)
    cnt = counts[0, :N_EXPERTS].astype(jnp.int32)
    padded = ((cnt + tm - 1) // tm) * tm
    ends = jnp.cumsum(padded)
    offs = ends - padded
    pos = offs[eid] + rank
    tile_start = jnp.arange(n_tiles, dtype=jnp.int32) * tm
    n_used = (ends[-1] // tm).astype(jnp.int32)
    probe = jnp.minimum(tile_start, ends[-1] - tm)
    te = jnp.sum((probe[:, None] >= ends[None, :]).astype(jnp.int32), axis=1)
    return dict(pos=pos, tile_expert=te, n_used=n_used.reshape(1), pad_start=offs + cnt,
                pad_n=padded - cnt, n_sorted=n_tiles * tm)


def _moe_and_ple(x, mix_out_kwargs, w_out, g_ffn, router, w_gu, w_down, p_i, ple):
    t = x.shape[0]
    x1, route, counts = _mix_out(x, w_out, g_ffn, *router, **mix_out_kwargs)
    lay = _sorted_layout(route, counts, t)
    xs = _dispatch(x1, g_ffn, lay["pos"], lay["pad_start"], lay["pad_n"], lay["n_used"], lay["n_sorted"])
    ys = _experts(xs, w_gu, w_down, lay["tile_expert"], lay["n_used"])
    return _ple(x1, ys, lay["pos"], route, p_i, *ple)


def kernel(x, p, norm_mix, norm_ffn, sb_w_in, sb_q_norm, sb_k_norm, sb_w_out, sg_w_in, sg_v_norm, sg_w_s, sg_b_s, sg_w_out, moe_w_group, moe_b_group, moe_w_expert, moe_b_expert, moe_w_gate, moe_w_up, moe_w_down, ple_norm_in, ple_w_gate, ple_w_proj, ple_norm_out):
    b, s, d = x.shape
    depth = norm_mix.shape[0]
    heads = d // LANES
    assert d == ROW_TILES * LANES
    t = b * s
    xt = x.reshape(t, d)
    row = lambda v: v.reshape(1, -1)
    for i in range(depth):
        j = i // 2
        router = _router_weights(moe_w_group[i], moe_b_group[i], moe_w_expert[i], moe_b_expert[i])
        w_gu = jnp.concatenate([moe_w_gate[i], moe_w_up[i]], axis=-1).astype(BF16)
        w_down = moe_w_down[i].astype(BF16)
        ple = (row(ple_norm_in[i]), ple_w_gate[i].astype(BF16), ple_w_proj[i].astype(BF16), row(ple_norm_out[i]))
        if i % 2 == 0:
            q_gain = sb_q_norm[j] * (LANES ** -0.5 * LOG2E)
            colgain = jnp.concatenate([jnp.tile(q_gain, heads), jnp.tile(sb_k_norm[j], heads),
                                       jnp.ones((d,), F32)]).reshape(1, 3 * d)
            qkv = _qkv_proj(xt, row(norm_mix[i]), sb_w_in[j].astype(BF16), colgain)
            o = _stick_breaking(qkv.reshape(b, s, 3 * d), heads).reshape(t, d)
            mix = dict(attn_out=o)
            w_out = sb_w_out[j]
        else:
            z = _sg_in_proj(xt, row(norm_mix[i]), sg_w_in[j].astype(BF16), row(sg_v_norm[j]))
            mix = dict(sg=(z, sg_w_s[j], sg_b_s[j].T))
            w_out = sg_w_out[j]
        xt = _moe_and_ple(xt, mix, w_out.astype(BF16), row(norm_ffn[i]), router, w_gu, w_down,
                          p[i].reshape(t, -1), ple)
    return xt.reshape(b, s, d)
```

```python
import functools

import jax
import jax.numpy as jnp
from jax import lax
from jax.experimental import pallas as pl
from jax.experimental.pallas import tpu as pltpu

F32 = jnp.float32
BF16 = jnp.bfloat16

LANES = 128
ROW_TILES = 16
VMEM_LIMIT_BYTES = 56 * 1024 * 1024
EPS = 1e-6
LOG2E = 1.4426950408889634
INV_LN2 = LOG2E

N_GROUPS = 4
EXPERTS_PER_GROUP = 8
N_EXPERTS = N_GROUPS * EXPERTS_PER_GROUP
ROUTER_GROUP_LANE0 = N_EXPERTS

TM_PROJ = 512
TM_ROW = 256
TQ = 256
HEADS_PER_STEP = 4
TM_EXPERT = 256


def _cparams(*sem):
    return pltpu.CompilerParams(dimension_semantics=sem, vmem_limit_bytes=VMEM_LIMIT_BYTES)


def _rms(x, g):
    ms = jnp.mean(x * x, axis=-1, keepdims=True)
    return x * lax.rsqrt(ms + EPS) * g


def _qkv_kernel(x_ref, g_ref, w_ref, cg_ref, o_ref, xn_ref, *, n_norm_tiles):
    j = pl.program_id(1)

    @pl.when(j == 0)
    def _():
        xn_ref[...] = _rms(x_ref[...], g_ref[...]).astype(BF16)

    acc = jnp.dot(xn_ref[...], w_ref[...], preferred_element_type=F32)

    @pl.when(j < n_norm_tiles)
    def _():
        for h in range(acc.shape[1] // LANES):
            sl = slice(h * LANES, (h + 1) * LANES)
            a = acc[:, sl]
            ms = jnp.mean(a * a, axis=-1, keepdims=True)
            o_ref[:, sl] = (a * lax.rsqrt(ms + EPS) * cg_ref[:, sl]).astype(BF16)

    @pl.when(j >= n_norm_tiles)
    def _():
        o_ref[...] = acc.astype(BF16)


def _qkv_proj(x, g, w, colgain, *, tn=1024):
    t, d = x.shape
    n = w.shape[1]
    tm = min(TM_PROJ, t)
    tn = min(tn, d)
    return pl.pallas_call(
        functools.partial(_qkv_kernel, n_norm_tiles=2 * d // tn),
        grid=(t // tm, n // tn),
        in_specs=[
            pl.BlockSpec((tm, d), lambda i, j: (i, 0)),
            pl.BlockSpec((1, d), lambda i, j: (0, 0)),
            pl.BlockSpec((d, tn), lambda i, j: (0, j)),
            pl.BlockSpec((1, tn), lambda i, j: (0, j)),
        ],
        out_specs=pl.BlockSpec((tm, tn), lambda i, j: (i, j)),
        out_shape=jax.ShapeDtypeStruct((t, n), BF16),
        scratch_shapes=[pltpu.VMEM((tm, d), BF16)],
        compiler_params=_cparams("parallel", "arbitrary"),
        name="qkv_proj",
    )(x, g, w, colgain)


def _sg_in_kernel(x_ref, g_ref, w_ref, vg_ref, o_ref, xn_ref):
    j = pl.program_id(1)

    @pl.when(j == 0)
    def _():
        xn_ref[...] = _rms(x_ref[...], g_ref[...]).astype(BF16)

    z = jax.nn.gelu(jnp.dot(xn_ref[...], w_ref[...], preferred_element_type=F32))

    @pl.when(j == 0)
    def _():
        o_ref[...] = z.astype(BF16)

    @pl.when(j == 1)
    def _():
        o_ref[...] = _rms(z, vg_ref[...]).astype(BF16)


def _sg_in_proj(x, g, w, v_gain):
    t, d = x.shape
    tm = min(TM_PROJ, t)
    return pl.pallas_call(
        _sg_in_kernel,
        grid=(t // tm, 2),
        in_specs=[
            pl.BlockSpec((tm, d), lambda i, j: (i, 0)),
            pl.BlockSpec((1, d), lambda i, j: (0, 0)),
            pl.BlockSpec((d, d), lambda i, j: (0, j)),
            pl.BlockSpec((1, d), lambda i, j: (0, 0)),
        ],
        out_specs=pl.BlockSpec((tm, d), lambda i, j: (i, j)),
        out_shape=jax.ShapeDtypeStruct((t, 2 * d), BF16),
        scratch_shapes=[pltpu.VMEM((tm, d), BF16)],
        compiler_params=_cparams("parallel", "arbitrary"),
        name="sg_in_proj",
    )(x, g, w, v_gain)


def _attn_kernel(q_ref, k_ref, v_ref, uu_ref, o_ref):
    qi = pl.program_id(2)
    tq = q_ref.shape[0]
    tk = uu_ref.shape[1]
    n_heads = q_ref.shape[1] // LANES
    head = lambda ref_or_val, h: ref_or_val[:, h * LANES:(h + 1) * LANES]
    qs = [head(q_ref, h) for h in range(n_heads)]

    def tile(kj, state, causal):
        rows_k = pl.ds(pl.multiple_of(kj * tk, tk), tk)
        k_all = k_ref[rows_k, :]
        v_all = v_ref[rows_k, :]
        heads = range(n_heads)
        zs = [lax.dot_general(qs[h], head(k_all, h), (((1,), (1,)), ((), ())), preferred_element_type=F32)
              for h in heads]
        cs = []
        for h in heads:
            z = zs[h]
            sp = jnp.maximum(z, 0.0) + jnp.log(1.0 + jnp.exp2(-jnp.abs(z))) * INV_LN2
            if causal is not None:
                sp = jnp.where(causal, sp, 0.0)
            hi = sp.astype(BF16)
            lo = (sp - hi.astype(F32)).astype(BF16)
            cs.append(jnp.dot(jnp.concatenate([hi, lo], axis=1), uu_ref[...], preferred_element_type=F32))
        out = []
        for h in heads:
            carry, acc = state[h]
            w = jnp.exp2(zs[h] - cs[h])
            if causal is not None:
                w = jnp.where(causal, w, 0.0)
            pv = jnp.dot(w.astype(BF16), head(v_all, h), preferred_element_type=F32)
            out.append((carry + cs[h][:, 0:1], acc + jnp.exp2(-carry) * pv))
        return tuple(out)

    rows = lax.broadcasted_iota(jnp.int32, (tq, tk), 0)
    cols = lax.broadcasted_iota(jnp.int32, (tq, tk), 1)
    state0 = tuple((jnp.zeros((tq, 1), F32), jnp.zeros((tq, LANES), F32)) for _ in range(n_heads))
    state = tile(qi, state0, cols < rows)
    state = lax.fori_loop(0, qi, lambda it, st: tile(qi - 1 - it, st, None), state)
    for h in range(n_heads):
        o_ref[:, h * LANES:(h + 1) * LANES] = state[h][1].astype(o_ref.dtype)


def _stick_breaking(qkv, heads):
    b, s, d3 = qkv.shape
    d = d3 // 3
    tq = min(TQ, s)
    tk = tq
    hw = HEADS_PER_STEP * LANES
    groups = heads // HEADS_PER_STEP
    j = lax.broadcasted_iota(jnp.int32, (tk, tk), 0)
    c = lax.broadcasted_iota(jnp.int32, (tk, tk), 1)
    u = (j >= c).astype(BF16)
    uu = jnp.concatenate([u, u], axis=0)
    return pl.pallas_call(
        _attn_kernel,
        grid=(b, groups, s // tq),
        in_specs=[
            pl.BlockSpec((None, tq, hw), lambda bi, h, i: (bi, i, h)),
            pl.BlockSpec((None, s, hw), lambda bi, h, i: (bi, 0, groups + h)),
            pl.BlockSpec((None, s, hw), lambda bi, h, i: (bi, 0, 2 * groups + h)),
            pl.BlockSpec((2 * tk, tk), lambda bi, h, i: (0, 0)),
        ],
        out_specs=pl.BlockSpec((None, tq, hw), lambda bi, h, i: (bi, i, h)),
        out_shape=jax.ShapeDtypeStruct((b, s, d), BF16),
        compiler_params=_cparams("parallel", "parallel", "arbitrary"),
        name="stick_breaking_attention",
    )(qkv, qkv, qkv, uu)


def _split_hi_lo(x):
    hi = x.astype(BF16)
    return hi, (x - hi.astype(F32)).astype(BF16)


def _route_tile(h2, wr_ref, br_ref, ltri_ref, cnt_ref):
    tm = h2.shape[0]
    hi, lo = _split_hi_lo(h2)
    lhs = jnp.concatenate([hi, lo, hi], axis=1)
    logits = jnp.dot(lhs, wr_ref[...], preferred_element_type=F32) + br_ref[...]

    lane = lax.broadcasted_iota(jnp.int32, (tm, LANES), 1)
    neg = jnp.float32(-jnp.inf)
    big = jnp.int32(LANES)

    is_group = (lane >= ROUTER_GROUP_LANE0) & (lane < ROUTER_GROUP_LANE0 + N_GROUPS)
    gl = jnp.where(is_group, logits, neg)
    gmax = jnp.max(gl, axis=-1, keepdims=True)
    g_idx = jnp.min(jnp.where(gl == gmax, lane - ROUTER_GROUP_LANE0, big), axis=-1, keepdims=True)
    g_w = 1.0 / jnp.sum(jnp.where(is_group, jnp.exp(gl - gmax), 0.0), axis=-1, keepdims=True)

    in_group = (lane < N_EXPERTS) & ((lane // EXPERTS_PER_GROUP) == g_idx)
    el = jnp.where(in_group, logits, neg)
    v1 = jnp.max(el, axis=-1, keepdims=True)
    i1 = jnp.min(jnp.where(el == v1, lane, big), axis=-1, keepdims=True)
    el2 = jnp.where(lane == i1, neg, el)
    v2 = jnp.max(el2, axis=-1, keepdims=True)
    i2 = jnp.min(jnp.where(el2 == v2, lane, big), axis=-1, keepdims=True)
    e21 = jnp.exp(v2 - v1)
    den = 1.0 + e21
    w1 = g_w * (1.0 / den)
    w2 = g_w * (e21 / den)

    oh1 = lane == i1
    oh2 = lane == i2
    onehot = (oh1 | oh2).astype(BF16)
    ahead = jnp.dot(ltri_ref[...], onehot, preferred_element_type=F32) + cnt_ref[...]
    r1 = jnp.sum(jnp.where(oh1, ahead, 0.0), axis=-1, keepdims=True)
    r2 = jnp.sum(jnp.where(oh2, ahead, 0.0), axis=-1, keepdims=True)
    cnt_ref[...] += jnp.sum(onehot.astype(F32), axis=0, keepdims=True)

    rec = jnp.where(lane == 0, i1.astype(F32), 0.0)
    rec = jnp.where(lane == 1, i2.astype(F32), rec)
    rec = jnp.where(lane == 2, r1, rec)
    rec = jnp.where(lane == 3, r2, rec)
    rec = jnp.where(lane == 4, w1, rec)
    rec = jnp.where(lane == 5, w2, rec)
    return rec


def _store_token_major(ref, val):
    tm = val.shape[0]
    n = val.shape[1] // LANES
    for c in range(n):
        ref[pl.ds(c, tm, stride=n), :] = val[:, c * LANES:(c + 1) * LANES]


def _load_token_major(ref, tm, n, c):
    return ref[pl.ds(c, tm, stride=n), :]


def _mix_out_epilogue(a, x_ref, wo_ref, gf_ref, wr_ref, br_ref, x1_ref, route_ref, cnt_out_ref,
                      ltri_ref, cnt_ref):
    tm = a.shape[0]

    @pl.when(pl.program_id(0) == 0)
    def _():
        r = lax.broadcasted_iota(jnp.int32, (tm, tm), 0)
        c = lax.broadcasted_iota(jnp.int32, (tm, tm), 1)
        ltri_ref[...] = (c < r).astype(BF16)
        cnt_ref[...] = jnp.zeros_like(cnt_ref)

    x1 = x_ref[...] + jnp.dot(a, wo_ref[...], preferred_element_type=F32)
    x1_ref[...] = x1
    route_ref[...] = _route_tile(_rms(x1, gf_ref[...]), wr_ref, br_ref, ltri_ref, cnt_ref)
    cnt_out_ref[...] = cnt_ref[...]


def _attn_out_kernel(a_ref, x_ref, wo_ref, gf_ref, wr_ref, br_ref,
                     x1_ref, route_ref, cnt_out_ref, ltri_ref, cnt_ref):
    _mix_out_epilogue(a_ref[...], x_ref, wo_ref, gf_ref, wr_ref, br_ref,
                      x1_ref, route_ref, cnt_out_ref, ltri_ref, cnt_ref)


def _sg_out_kernel(u_ref, vn_ref, ws_ref, bs_ref, x_ref, wo_ref, gf_ref, wr_ref, br_ref,
                   x1_ref, route_ref, cnt_out_ref, ltri_ref, cnt_ref, a_ref):
    tm, d = u_ref.shape
    r = lax.broadcasted_iota(jnp.int32, (LANES, LANES), 0)
    c = lax.broadcasted_iota(jnp.int32, (LANES, LANES), 1)
    keep = c <= r
    for g in range(d // LANES):
        sl = slice(g * LANES, (g + 1) * LANES)
        wc = jnp.where(keep, ws_ref[g], 0.0).astype(BF16)
        bias = bs_ref[:, g:g + 1]
        for ch in range(tm // LANES):
            rs = slice(ch * LANES, (ch + 1) * LANES)
            mixed = jnp.dot(wc, vn_ref[rs, sl], preferred_element_type=F32) + bias
            a_ref[rs, sl] = (u_ref[rs, sl].astype(F32) * mixed).astype(BF16)
    _mix_out_epilogue(a_ref[...], x_ref, wo_ref, gf_ref, wr_ref, br_ref,
                      x1_ref, route_ref, cnt_out_ref, ltri_ref, cnt_ref)


def _mix_out(x, w_out, g_ffn, w_router, b_router, *, attn_out=None, sg=None):
    t, d = x.shape
    tm = min(TM_ROW, t)
    row_spec = pl.BlockSpec((tm, d), lambda i: (i, 0))
    const = lambda shape: pl.BlockSpec(shape, lambda i: (0,) * len(shape))
    common_in = [row_spec, const((d, d)), const((1, d)), const(w_router.shape), const((1, LANES))]
    common_args = (x, w_out, g_ffn, w_router, b_router)
    out_specs = [row_spec, pl.BlockSpec((tm, LANES), lambda i: (i, 0)), const((1, LANES))]
    out_shape = [
        jax.ShapeDtypeStruct((t, d), F32),
        jax.ShapeDtypeStruct((t, LANES), F32),
        jax.ShapeDtypeStruct((1, LANES), F32),
    ]
    scratch = [pltpu.VMEM((tm, tm), BF16), pltpu.VMEM((1, LANES), F32)]
    if attn_out is not None:
        return pl.pallas_call(
            _attn_out_kernel,
            grid=(t // tm,),
            in_specs=[row_spec] + common_in,
            out_specs=out_specs, out_shape=out_shape, scratch_shapes=scratch,
            compiler_params=_cparams("arbitrary"),
            name="attn_out_router",
        )(attn_out, *common_args)
    z, w_s, b_s_t = sg
    return pl.pallas_call(
        _sg_out_kernel,
        grid=(t // tm,),
        in_specs=[
            pl.BlockSpec((tm, d), lambda i: (i, 0)),
            pl.BlockSpec((tm, d), lambda i: (i, 1)),
            const(w_s.shape),
            const(b_s_t.shape),
        ] + common_in,
        out_specs=out_specs, out_shape=out_shape,
        scratch_shapes=scratch + [pltpu.VMEM((tm, d), BF16)],
        compiler_params=_cparams("arbitrary"),
        name="sg_out_router",
    )(z, z, w_s, b_s_t, *common_args)


def _row_slice(ref, row):
    return ref.at[pl.ds(pl.multiple_of(row * ROW_TILES, ROW_TILES), ROW_TILES)]


def _dispatch_kernel(pos_ref, pad_start_ref, pad_n_ref, n_used_ref, x1_ref, g_ref, xs_ref,
                     f32_ref, stage_ref, zero_ref, sems, pad_sem):
    i = pl.program_id(0)
    n_steps = pl.num_programs(0)
    td = x1_ref.shape[0]
    slot = i % 2

    def wait_slot(s):
        for _ in range(2):
            pltpu.make_async_copy(stage_ref.at[s], xs_ref.at[pl.ds(0, td * ROW_TILES)], sems.at[s]).wait()

    @pl.when(i == 0)
    def _():
        zero_ref[...] = jnp.zeros_like(zero_ref)
        tile_rows = zero_ref.shape[0]
        n_tiles = xs_ref.shape[0] // tile_rows

        def pad_copy(e, r):
            return pltpu.make_async_copy(zero_ref.at[pl.ds(0, ROW_TILES)],
                                         _row_slice(xs_ref, pad_start_ref[e] + r), pad_sem)

        def tile_copy(j):
            return pltpu.make_async_copy(
                zero_ref, xs_ref.at[pl.ds(pl.multiple_of(j * tile_rows, tile_rows), tile_rows)], pad_sem)

        def start_all(e, carry):
            lax.fori_loop(0, pad_n_ref[e], lambda r, c: (pad_copy(e, r).start(), c)[1], 0)
            return carry

        def wait_all(e, carry):
            lax.fori_loop(0, pad_n_ref[e], lambda r, c: (pad_copy(e, r).wait(), c)[1], 0)
            return carry

        lax.fori_loop(0, pad_n_ref.shape[0], start_all, 0)
        lax.fori_loop(n_used_ref[0], n_tiles, lambda j, c: (tile_copy(j).start(), c)[1], 0)
        lax.fori_loop(0, pad_n_ref.shape[0], wait_all, 0)
        lax.fori_loop(n_used_ref[0], n_tiles, lambda j, c: (tile_copy(j).wait(), c)[1], 0)

    @pl.when(i >= 2)
    def _():
        wait_slot(slot)

    _store_token_major(f32_ref, _rms(x1_ref[...], g_ref[...]))
    stage_ref[slot] = f32_ref[...].astype(BF16)

    def issue(t, carry):
        src = stage_ref.at[slot, pl.ds(pl.multiple_of(t * ROW_TILES, ROW_TILES), ROW_TILES)]
        for k in range(2):
            pltpu.make_async_copy(src, _row_slice(xs_ref, pos_ref[0, 0, 2 * t + k]), sems.at[slot]).start()
        return carry

    lax.fori_loop(0, td, issue, 0)

    @pl.when(i == n_steps - 1)
    def _():
        wait_slot(slot)

    @pl.when((i == n_steps - 1) & (i >= 1))
    def _():
        wait_slot(1 - slot)


def _dispatch(x1, g_ffn, pos, pad_start, pad_n, n_used, n_sorted_rows):
    t, d = x1.shape
    td = min(TM_ROW, t)
    pos3 = pos.reshape(t // td, 1, 2 * td)
    smem = pl.BlockSpec(memory_space=pltpu.SMEM)
    return pl.pallas_call(
        _dispatch_kernel,
        grid=(t // td,),
        in_specs=[
            pl.BlockSpec((1, 1, 2 * td), lambda i: (i, 0, 0), memory_space=pltpu.SMEM),
            smem, smem, smem,
            pl.BlockSpec((td, d), lambda i: (i, 0)),
            pl.BlockSpec((1, d), lambda i: (0, 0)),
        ],
        out_specs=pl.BlockSpec(memory_space=pl.ANY),
        out_shape=jax.ShapeDtypeStruct((n_sorted_rows * ROW_TILES, LANES), BF16),
        scratch_shapes=[
            pltpu.VMEM((td * ROW_TILES, LANES), F32),
            pltpu.VMEM((2, td * ROW_TILES, LANES), BF16),
            pltpu.VMEM((TM_EXPERT * ROW_TILES, LANES), BF16),
            pltpu.SemaphoreType.DMA((2,)),
            pltpu.SemaphoreType.DMA(()),
        ],
        compiler_params=_cparams("arbitrary"),
        name="moe_dispatch",
    )(pos3, pad_start, pad_n, n_used, x1, g_ffn)


def _expert_kernel(tile_expert_ref, n_used_ref, xs_ref, wgu_ref, wd_ref, ys_ref, f32_ref, a_ref):
    i = pl.program_id(0)
    tm, d = a_ref.shape
    n_row = d // LANES
    f = wd_ref.shape[0]

    @pl.when(i < n_used_ref[0])
    def _():
        f32_ref[...] = xs_ref[...].astype(F32)
        for c in range(n_row):
            a_ref[:, c * LANES:(c + 1) * LANES] = _load_token_major(f32_ref, tm, n_row, c).astype(BF16)
        gu = jnp.dot(a_ref[...], wgu_ref[...], preferred_element_type=F32)
        hidden = (jax.nn.silu(gu[:, :f]) * gu[:, f:]).astype(BF16)
        _store_token_major(f32_ref, jnp.dot(hidden, wd_ref[...], preferred_element_type=F32))
        ys_ref[...] = f32_ref[...].astype(BF16)

    @pl.when(i >= n_used_ref[0])
    def _():
        ys_ref[...] = jnp.zeros_like(ys_ref)


def _experts(xs, w_gu, w_down, tile_expert, n_used):
    e, d, f2 = w_gu.shape
    n_rows = xs.shape[0] // ROW_TILES
    tm = TM_EXPERT
    n_tiles = n_rows // tm
    return pl.pallas_call(
        _expert_kernel,
        grid_spec=pltpu.PrefetchScalarGridSpec(
            num_scalar_prefetch=2,
            grid=(n_tiles,),
            in_specs=[
                pl.BlockSpec((tm * ROW_TILES, LANES), lambda i, te, nu: (jnp.minimum(i, nu[0] - 1), 0)),
                pl.BlockSpec((None, d, f2), lambda i, te, nu: (te[i], 0, 0)),
                pl.BlockSpec((None, f2 // 2, d), lambda i, te, nu: (te[i], 0, 0)),
            ],
            out_specs=pl.BlockSpec((tm * ROW_TILES, LANES), lambda i, te, nu: (i, 0)),
            scratch_shapes=[pltpu.VMEM((tm * ROW_TILES, LANES), F32), pltpu.VMEM((tm, d), BF16)],
        ),
        out_shape=jax.ShapeDtypeStruct(xs.shape, BF16),
        compiler_params=_cparams("arbitrary"),
        name="moe_experts",
    )(tile_expert, n_used, xs, w_gu, w_down)


def _ple_kernel(pos_ref, pos_next_ref, x1_ref, route_ref, p_ref, gin_ref, wg_ref, wp_ref, gout_ref, ys_ref,
                o_ref, ybuf_ref, f32_ref, x2_ref, sems):
    i = pl.program_id(0)
    n_steps = pl.num_programs(0)
    tm, d = x1_ref.shape
    n_row = d // LANES
    slot = i % 2

    def gather(p_ref_, s):
        def issue(t, carry):
            for k in range(2):
                dst = ybuf_ref.at[s, k, pl.ds(pl.multiple_of(t * ROW_TILES, ROW_TILES), ROW_TILES)]
                pltpu.make_async_copy(_row_slice(ys_ref, p_ref_[0, 0, 2 * t + k]), dst, sems.at[s]).start()
            return carry
        lax.fori_loop(0, tm, issue, 0)

    @pl.when(i == 0)
    def _():
        gather(pos_ref, 0)

    @pl.when(i + 1 < n_steps)
    def _():
        gather(pos_next_ref, 1 - slot)

    for k in range(2):
        pltpu.make_async_copy(ys_ref.at[pl.ds(0, tm * ROW_TILES)], ybuf_ref.at[slot, k], sems.at[slot]).wait()

    w = (route_ref[:, 4:5], route_ref[:, 5:6])
    for k in range(2):
        f32_ref[k] = ybuf_ref[slot, k].astype(F32)
    for c in range(n_row):
        sl = slice(c * LANES, (c + 1) * LANES)
        y0 = _load_token_major(f32_ref.at[0], tm, n_row, c)
        y1 = _load_token_major(f32_ref.at[1], tm, n_row, c)
        x2_ref[:, sl] = x1_ref[:, sl] + w[0] * y0 + w[1] * y1
    x2 = x2_ref[...]
    gate = jax.nn.sigmoid(jnp.dot(_rms(x2, gin_ref[...]).astype(BF16), wg_ref[...],
                                  preferred_element_type=F32))
    e = jnp.dot(p_ref[...].astype(BF16), wp_ref[...], preferred_element_type=F32) * gate
    o_ref[...] = x2 + _rms(e, gout_ref[...])


def _ple(x1, ys, pos, route, p, g_in, w_gate, w_proj, g_out):
    t, d = x1.shape
    tm = min(TM_ROW, t)
    n_steps = t // tm
    pd = p.shape[1]
    pos3 = pos.reshape(n_steps, 1, 2 * tm)
    row_spec = pl.BlockSpec((tm, d), lambda i: (i, 0))
    const = lambda shape: pl.BlockSpec(shape, lambda i: (0,) * len(shape))
    return pl.pallas_call(
        _ple_kernel,
        grid=(n_steps,),
        in_specs=[
            pl.BlockSpec((1, 1, 2 * tm), lambda i: (i, 0, 0), memory_space=pltpu.SMEM),
            pl.BlockSpec((1, 1, 2 * tm), lambda i: (jnp.minimum(i + 1, n_steps - 1), 0, 0),
                         memory_space=pltpu.SMEM),
            row_spec,
            pl.BlockSpec((tm, LANES), lambda i: (i, 0)),
            pl.BlockSpec((tm, pd), lambda i: (i, 0)),
            const((1, d)), const((d, d)), const((pd, d)), const((1, d)),
            pl.BlockSpec(memory_space=pl.ANY),
        ],
        out_specs=row_spec,
        out_shape=jax.ShapeDtypeStruct((t, d), F32),
        scratch_shapes=[
            pltpu.VMEM((2, 2, tm * ROW_TILES, LANES), BF16),
            pltpu.VMEM((2, tm * ROW_TILES, LANES), F32),
            pltpu.VMEM((tm, d), F32),
            pltpu.SemaphoreType.DMA((2,)),
        ],
        compiler_params=_cparams("arbitrary"),
        name="moe_combine_ple",
    )(pos3, pos3, x1, route, p, g_in, w_gate, w_proj, g_out, ys)


def _router_weights(w_group, b_group, w_expert, b_expert):
    d = w_group.shape[0]
    w = jnp.zeros((d, LANES), F32)
    w = w.at[:, :N_EXPERTS].set(w_expert).at[:, ROUTER_GROUP_LANE0:ROUTER_GROUP_LANE0 + N_GROUPS].set(w_group)
    b = jnp.zeros((1, LANES), F32)
    b = b.at[0, :N_EXPERTS].set(b_expert).at[0, ROUTER_GROUP_LANE0:ROUTER_GROUP_LANE0 + N_GROUPS].set(b_group)
    w_hi = w.astype(BF16)
    w_lo = (w - w_hi.astype(F32)).astype(BF16)
    return jnp.concatenate([w_hi, w_hi, w_lo], axis=0), b


def _sorted_layout(route, counts, t):
    tm = TM_EXPERT
    n_tiles = (2 * t + N_EXPERTS * (tm - 1)) // tm + 1
    eid = route[:, 0:2].astype(jnp.int32)
    rank = route[:, 2:4].astype(jnp.int32)
    cnt = counts[0, :N_EXPERTS].astype(jnp.int32)
    padded = ((cnt + tm - 1) // tm) * tm
    ends = jnp.cumsum(padded)
    offs = ends - padded
    pos = offs[eid] + rank
    tile_start = jnp.arange(n_tiles, dtype=jnp.int32) * tm
    n_used = (ends[-1] // tm).astype(jnp.int32)
    probe = jnp.minimum(tile_start, ends[-1] - tm)
    te = jnp.sum((probe[:, None] >= ends[None, :]).astype(jnp.int32), axis=1)
    return dict(pos=pos, tile_expert=te, n_used=n_used.reshape(1), pad_start=offs + cnt,
                pad_n=padded - cnt, n_sorted=n_tiles * tm)


def _moe_and_ple(x, mix_out_kwargs, w_out, g_ffn, router, w_gu, w_down, p_i, ple):
    t = x.shape[0]
    x1, route, counts = _mix_out(x, w_out, g_ffn, *router, **mix_out_kwargs)
    lay = _sorted_layout(route, counts, t)
    xs = _dispatch(x1, g_ffn, lay["pos"], lay["pad_start"], lay["pad_n"], lay["n_used"], lay["n_sorted"])
    ys = _experts(xs, w_gu, w_down, lay["tile_expert"], lay["n_used"])
    return _ple(x1, ys, lay["pos"], route, p_i, *ple)


def kernel(x, p, norm_mix, norm_ffn, sb_w_in, sb_q_norm, sb_k_norm, sb_w_out, sg_w_in, sg_v_norm, sg_w_s, sg_b_s, sg_w_out, moe_w_group, moe_b_group, moe_w_expert, moe_b_expert, moe_w_gate, moe_w_up, moe_w_down, ple_norm_in, ple_w_gate, ple_w_proj, ple_norm_out):
    b, s, d = x.shape
    depth = norm_mix.shape[0]
    heads = d // LANES
    assert d == ROW_TILES * LANES
    t = b * s
    xt = x.reshape(t, d)
    row = lambda v: v.reshape(1, -1)
    for i in range(depth):
        j = i // 2
        router = _router_weights(moe_w_group[i], moe_b_group[i], moe_w_expert[i], moe_b_expert[i])
        w_gu = jnp.concatenate([moe_w_gate[i], moe_w_up[i]], axis=-1).astype(BF16)
        w_down = moe_w_down[i].astype(BF16)
        ple = (row(ple_norm_in[i]), ple_w_gate[i].astype(BF16), ple_w_proj[i].astype(BF16), row(ple_norm_out[i]))
        if i % 2 == 0:
            q_gain = sb_q_norm[j] * (LANES ** -0.5 * LOG2E)
            colgain = jnp.concatenate([jnp.tile(q_gain, heads), jnp.tile(sb_k_norm[j], heads),
                                       jnp.ones((d,), F32)]).reshape(1, 3 * d)
            qkv = _qkv_proj(xt, row(norm_mix[i]), sb_w_in[j].astype(BF16), colgain)
            o = _stick_breaking(qkv.reshape(b, s, 3 * d), heads).reshape(t, d)
            mix = dict(attn_out=o)
            w_out = sb_w_out[j]
        else:
            z = _sg_in_proj(xt, row(norm_mix[i]), sg_w_in[j].astype(BF16), row(sg_v_norm[j]))
            mix = dict(sg=(z, sg_w_s[j], sg_b_s[j].T))
            w_out = sg_w_out[j]
        xt = _moe_and_ple(xt, mix, w_out.astype(BF16), row(norm_ffn[i]), router, w_gu, w_down,
                          p[i].reshape(t, -1), ple)
    return xt.reshape(b, s, d)
```

```python
import functools

import jax
import jax.numpy as jnp
from jax import lax
from jax.experimental import pallas as pl
from jax.experimental.pallas import tpu as pltpu

F32 = jnp.float32
BF16 = jnp.bfloat16

LANES = 128
ROW_TILES = 16
VMEM_LIMIT_BYTES = 56 * 1024 * 1024
EPS = 1e-6
LOG2E = 1.4426950408889634
INV_LN2 = LOG2E
SIGN_BIT = -2 ** 31

N_GROUPS = 4
EXPERTS_PER_GROUP = 8
N_EXPERTS = N_GROUPS * EXPERTS_PER_GROUP
ROUTER_GROUP_LANE0 = N_EXPERTS

TM_PROJ = 512
TM_ROW = 256
TQ = 256
HEADS_PER_STEP = 4
TM_EXPERT = 256


def _cparams(*sem):
    return pltpu.CompilerParams(dimension_semantics=sem, vmem_limit_bytes=VMEM_LIMIT_BYTES)


def _rms(x, g):
    ms = jnp.mean(x * x, axis=-1, keepdims=True)
    return x * lax.rsqrt(ms + EPS) * g


def _qkv_kernel(x_ref, g_ref, w_ref, cg_ref, o_ref, xn_ref, *, n_norm_tiles):
    j = pl.program_id(1)

    @pl.when(j == 0)
    def _():
        xn_ref[...] = _rms(x_ref[...], g_ref[...]).astype(BF16)

    acc = jnp.dot(xn_ref[...], w_ref[...], preferred_element_type=F32)

    @pl.when(j < n_norm_tiles)
    def _():
        for h in range(acc.shape[1] // LANES):
            sl = slice(h * LANES, (h + 1) * LANES)
            a = acc[:, sl]
            ms = jnp.mean(a * a, axis=-1, keepdims=True)
            o_ref[:, sl] = (a * lax.rsqrt(ms + EPS) * cg_ref[:, sl]).astype(BF16)

    @pl.when(j >= n_norm_tiles)
    def _():
        o_ref[...] = acc.astype(BF16)


def _qkv_proj(x, g, w, colgain, *, tn=1024):
    t, d = x.shape
    n = w.shape[1]
    tm = min(TM_PROJ, t)
    tn = min(tn, d)
    return pl.pallas_call(
        functools.partial(_qkv_kernel, n_norm_tiles=2 * d // tn),
        grid=(t // tm, n // tn),
        in_specs=[
            pl.BlockSpec((tm, d), lambda i, j: (i, 0)),
            pl.BlockSpec((1, d), lambda i, j: (0, 0)),
            pl.BlockSpec((d, tn), lambda i, j: (0, j)),
            pl.BlockSpec((1, tn), lambda i, j: (0, j)),
        ],
        out_specs=pl.BlockSpec((tm, tn), lambda i, j: (i, j)),
        out_shape=jax.ShapeDtypeStruct((t, n), BF16),
        scratch_shapes=[pltpu.VMEM((tm, d), BF16)],
        compiler_params=_cparams("parallel", "arbitrary"),
        name="qkv_proj",
    )(x, g, w, colgain)


def _sg_in_kernel(x_ref, g_ref, w_ref, vg_ref, o_ref, xn_ref):
    j = pl.program_id(1)

    @pl.when(j == 0)
    def _():
        xn_ref[...] = _rms(x_ref[...], g_ref[...]).astype(BF16)

    z = jax.nn.gelu(jnp.dot(xn_ref[...], w_ref[...], preferred_element_type=F32))

    @pl.when(j == 0)
    def _():
        o_ref[...] = z.astype(BF16)

    @pl.when(j == 1)
    def _():
        o_ref[...] = _rms(z, vg_ref[...]).astype(BF16)


def _sg_in_proj(x, g, w, v_gain):
    t, d = x.shape
    tm = min(TM_PROJ, t)
    return pl.pallas_call(
        _sg_in_kernel,
        grid=(t // tm, 2),
        in_specs=[
            pl.BlockSpec((tm, d), lambda i, j: (i, 0)),
            pl.BlockSpec((1, d), lambda i, j: (0, 0)),
            pl.BlockSpec((d, d), lambda i, j: (0, j)),
            pl.BlockSpec((1, d), lambda i, j: (0, 0)),
        ],
        out_specs=pl.BlockSpec((tm, d), lambda i, j: (i, j)),
        out_shape=jax.ShapeDtypeStruct((t, 2 * d), BF16),
        scratch_shapes=[pltpu.VMEM((tm, d), BF16)],
        compiler_params=_cparams("parallel", "arbitrary"),
        name="sg_in_proj",
    )(x, g, w, v_gain)


def _attn_kernel(q_ref, k_ref, v_ref, uu_ref, o_ref, acc_ref, carry_ref, z_ref):
    qi = pl.program_id(2)
    tq = q_ref.shape[0]
    tk = uu_ref.shape[1]
    n_heads = q_ref.shape[1] // LANES
    heads = range(n_heads)
    head = lambda ref_or_val, h: ref_or_val[:, h * LANES:(h + 1) * LANES]
    qs = [head(q_ref, h) for h in heads]

    def key_rows(kj):
        return pl.ds(pl.multiple_of(kj * tk, tk), tk)

    def store_logits(kj, slot):
        k_all = k_ref[key_rows(kj), :]
        for h in heads:
            z_ref[slot, h] = lax.dot_general(qs[h], head(k_all, h), (((1,), (1,)), ((), ())),
                                             preferred_element_type=F32)

    def neg_abs(z):
        return lax.bitcast_convert_type(lax.bitcast_convert_type(z, jnp.int32) | SIGN_BIT, F32)

    def tile(kj, slot, causal):
        v_all = v_ref[key_rows(kj), :]
        cs = []
        for h in heads:
            z = z_ref[slot, h]
            sp = jnp.maximum(z, 0.0) + jnp.log(1.0 + jnp.exp2(neg_abs(z))) * INV_LN2
            if causal is not None:
                sp = jnp.where(causal, sp, 0.0)
            cs.append(jnp.dot(sp.astype(BF16), uu_ref[...], preferred_element_type=F32))
        for h in heads:
            w = jnp.exp2(z_ref[slot, h] - cs[h])
            if causal is not None:
                w = jnp.where(causal, w, 0.0)
            pv = jnp.dot(w.astype(BF16), head(v_all, h), preferred_element_type=F32)
            carry = carry_ref[h]
            acc_ref[h] += jnp.exp2(-carry) * pv
            carry_ref[h] = carry + cs[h][:, 0:1]
        store_logits(jnp.maximum(kj - 1, 0), 1 - slot)

    acc_ref[...] = jnp.zeros_like(acc_ref)
    carry_ref[...] = jnp.zeros_like(carry_ref)
    rows = lax.broadcasted_iota(jnp.int32, (tq, tk), 0)
    cols = lax.broadcasted_iota(jnp.int32, (tq, tk), 1)
    store_logits(qi, 0)
    tile(qi, 0, cols < rows)

    def body(it, c):
        tile(qi - 1 - it, (it + 1) % 2, None)
        return c

    lax.fori_loop(0, qi, body, 0)
    for h in heads:
        o_ref[:, h * LANES:(h + 1) * LANES] = acc_ref[h].astype(o_ref.dtype)


def _stick_breaking(qkv, heads):
    b, s, d3 = qkv.shape
    d = d3 // 3
    tq = min(TQ, s)
    tk = tq
    hw = HEADS_PER_STEP * LANES
    groups = heads // HEADS_PER_STEP
    j = lax.broadcasted_iota(jnp.int32, (tk, tk), 0)
    c = lax.broadcasted_iota(jnp.int32, (tk, tk), 1)
    uu = (j >= c).astype(BF16)
    return pl.pallas_call(
        _attn_kernel,
        grid=(b, groups, s // tq),
        in_specs=[
            pl.BlockSpec((None, tq, hw), lambda bi, h, i: (bi, i, h)),
            pl.BlockSpec((None, s, hw), lambda bi, h, i: (bi, 0, groups + h)),
            pl.BlockSpec((None, s, hw), lambda bi, h, i: (bi, 0, 2 * groups + h)),
            pl.BlockSpec((tk, tk), lambda bi, h, i: (0, 0)),
        ],
        out_specs=pl.BlockSpec((None, tq, hw), lambda bi, h, i: (bi, i, h)),
        out_shape=jax.ShapeDtypeStruct((b, s, d), BF16),
        scratch_shapes=[pltpu.VMEM((HEADS_PER_STEP, tq, LANES), F32),
                        pltpu.VMEM((HEADS_PER_STEP, tq, LANES), F32),
                        pltpu.VMEM((2, HEADS_PER_STEP, tq, tk), F32)],
        compiler_params=_cparams("parallel", "parallel", "arbitrary"),
        name="stick_breaking_attention",
    )(qkv, qkv, qkv, uu)


def _split_hi_lo(x):
    hi = x.astype(BF16)
    return hi, (x - hi.astype(F32)).astype(BF16)


def _route_tile(h2, wr_ref, br_ref, ltri_ref, cnt_ref):
    tm = h2.shape[0]
    hi, lo = _split_hi_lo(h2)
    lhs = jnp.concatenate([hi, lo, hi], axis=1)
    logits = jnp.dot(lhs, wr_ref[...], preferred_element_type=F32) + br_ref[...]

    lane = lax.broadcasted_iota(jnp.int32, (tm, LANES), 1)
    neg = jnp.float32(-jnp.inf)
    big = jnp.int32(LANES)

    is_group = (lane >= ROUTER_GROUP_LANE0) & (lane < ROUTER_GROUP_LANE0 + N_GROUPS)
    gl = jnp.where(is_group, logits, neg)
    gmax = jnp.max(gl, axis=-1, keepdims=True)
    g_idx = jnp.min(jnp.where(gl == gmax, lane - ROUTER_GROUP_LANE0, big), axis=-1, keepdims=True)
    g_w = 1.0 / jnp.sum(jnp.where(is_group, jnp.exp(gl - gmax), 0.0), axis=-1, keepdims=True)

    in_group = (lane < N_EXPERTS) & ((lane // EXPERTS_PER_GROUP) == g_idx)
    el = jnp.where(in_group, logits, neg)
    v1 = jnp.max(el, axis=-1, keepdims=True)
    i1 = jnp.min(jnp.where(el == v1, lane, big), axis=-1, keepdims=True)
    el2 = jnp.where(lane == i1, neg, el)
    v2 = jnp.max(el2, axis=-1, keepdims=True)
    i2 = jnp.min(jnp.where(el2 == v2, lane, big), axis=-1, keepdims=True)
    e21 = jnp.exp(v2 - v1)
    den = 1.0 + e21
    w1 = g_w * (1.0 / den)
    w2 = g_w * (e21 / den)

    oh1 = lane == i1
    oh2 = lane == i2
    onehot = (oh1 | oh2).astype(BF16)
    ahead = jnp.dot(ltri_ref[...], onehot, preferred_element_type=F32) + cnt_ref[...]
    r1 = jnp.sum(jnp.where(oh1, ahead, 0.0), axis=-1, keepdims=True)
    r2 = jnp.sum(jnp.where(oh2, ahead, 0.0), axis=-1, keepdims=True)
    cnt_ref[...] += jnp.sum(onehot.astype(F32), axis=0, keepdims=True)

    rec = jnp.where(lane == 0, i1.astype(F32), 0.0)
    rec = jnp.where(lane == 1, i2.astype(F32), rec)
    rec = jnp.where(lane == 2, r1, rec)
    rec = jnp.where(lane == 3, r2, rec)
    rec = jnp.where(lane == 4, w1, rec)
    rec = jnp.where(lane == 5, w2, rec)
    return rec


def _store_token_major(ref, val):
    tm = val.shape[0]
    n = val.shape[1] // LANES
    for c in range(n):
        ref[pl.ds(c, tm, stride=n), :] = val[:, c * LANES:(c + 1) * LANES]


def _load_token_major(ref, tm, n, c):
    return ref[pl.ds(c, tm, stride=n), :]


def _mix_out_epilogue(a, x_ref, wo_ref, gf_ref, wr_ref, br_ref, x1_ref, route_ref, cnt_out_ref,
                      ltri_ref, cnt_ref):
    tm = a.shape[0]

    @pl.when(pl.program_id(0) == 0)
    def _():
        r = lax.broadcasted_iota(jnp.int32, (tm, tm), 0)
        c = lax.broadcasted_iota(jnp.int32, (tm, tm), 1)
        ltri_ref[...] = (c < r).astype(BF16)
        cnt_ref[...] = jnp.zeros_like(cnt_ref)

    x1 = x_ref[...] + jnp.dot(a, wo_ref[...], preferred_element_type=F32)
    x1_ref[...] = x1
    route_ref[...] = _route_tile(_rms(x1, gf_ref[...]), wr_ref, br_ref, ltri_ref, cnt_ref)
    cnt_out_ref[...] = cnt_ref[...]


def _attn_out_kernel(a_ref, x_ref, wo_ref, gf_ref, wr_ref, br_ref,
                     x1_ref, route_ref, cnt_out_ref, ltri_ref, cnt_ref):
    _mix_out_epilogue(a_ref[...], x_ref, wo_ref, gf_ref, wr_ref, br_ref,
                      x1_ref, route_ref, cnt_out_ref, ltri_ref, cnt_ref)


def _sg_out_kernel(u_ref, vn_ref, ws_ref, bs_ref, x_ref, wo_ref, gf_ref, wr_ref, br_ref,
                   x1_ref, route_ref, cnt_out_ref, ltri_ref, cnt_ref, a_ref):
    tm, d = u_ref.shape
    r = lax.broadcasted_iota(jnp.int32, (LANES, LANES), 0)
    c = lax.broadcasted_iota(jnp.int32, (LANES, LANES), 1)
    keep = c <= r
    for g in range(d // LANES):
        sl = slice(g * LANES, (g + 1) * LANES)
        wc = jnp.where(keep, ws_ref[g], 0.0).astype(BF16)
        bias = bs_ref[:, g:g + 1]
        for ch in range(tm // LANES):
            rs = slice(ch * LANES, (ch + 1) * LANES)
            mixed = jnp.dot(wc, vn_ref[rs, sl], preferred_element_type=F32) + bias
            a_ref[rs, sl] = (u_ref[rs, sl].astype(F32) * mixed).astype(BF16)
    _mix_out_epilogue(a_ref[...], x_ref, wo_ref, gf_ref, wr_ref, br_ref,
                      x1_ref, route_ref, cnt_out_ref, ltri_ref, cnt_ref)


def _mix_out(x, w_out, g_ffn, w_router, b_router, *, attn_out=None, sg=None):
    t, d = x.shape
    tm = min(TM_ROW, t)
    row_spec = pl.BlockSpec((tm, d), lambda i: (i, 0))
    const = lambda shape: pl.BlockSpec(shape, lambda i: (0,) * len(shape))
    common_in = [row_spec, const((d, d)), const((1, d)), const(w_router.shape), const((1, LANES))]
    common_args = (x, w_out, g_ffn, w_router, b_router)
    out_specs = [row_spec, pl.BlockSpec((tm, LANES), lambda i: (i, 0)), const((1, LANES))]
    out_shape = [
        jax.ShapeDtypeStruct((t, d), F32),
        jax.ShapeDtypeStruct((t, LANES), F32),
        jax.ShapeDtypeStruct((1, LANES), F32),
    ]
    scratch = [pltpu.VMEM((tm, tm), BF16), pltpu.VMEM((1, LANES), F32)]
    if attn_out is not None:
        return pl.pallas_call(
            _attn_out_kernel,
            grid=(t // tm,),
            in_specs=[row_spec] + common_in,
            out_specs=out_specs, out_shape=out_shape, scratch_shapes=scratch,
            compiler_params=_cparams("arbitrary"),
            name="attn_out_router",
        )(attn_out, *common_args)
    z, w_s, b_s_t = sg
    return pl.pallas_call(
        _sg_out_kernel,
        grid=(t // tm,),
        in_specs=[
            pl.BlockSpec((tm, d), lambda i: (i, 0)),
            pl.BlockSpec((tm, d), lambda i: (i, 1)),
            const(w_s.shape),
            const(b_s_t.shape),
        ] + common_in,
        out_specs=out_specs, out_shape=out_shape,
        scratch_shapes=scratch + [pltpu.VMEM((tm, d), BF16)],
        compiler_params=_cparams("arbitrary"),
        name="sg_out_router",
    )(z, z, w_s, b_s_t, *common_args)


def _row_slice(ref, row):
    return ref.at[pl.ds(pl.multiple_of(row * ROW_TILES, ROW_TILES), ROW_TILES)]


def _dispatch_kernel(pos_ref, pad_start_ref, pad_n_ref, n_used_ref, x1_ref, g_ref, xs_ref,
                     f32_ref, stage_ref, zero_ref, sems, pad_sem):
    i = pl.program_id(0)
    n_steps = pl.num_programs(0)
    td = x1_ref.shape[0]
    slot = i % 2

    def wait_slot(s):
        for _ in range(2):
            pltpu.make_async_copy(stage_ref.at[s], xs_ref.at[pl.ds(0, td * ROW_TILES)], sems.at[s]).wait()

    @pl.when(i == 0)
    def _():
        zero_ref[...] = jnp.zeros_like(zero_ref)
        tile_rows = zero_ref.shape[0]
        n_tiles = xs_ref.shape[0] // tile_rows

        def pad_copy(e, r):
            return pltpu.make_async_copy(zero_ref.at[pl.ds(0, ROW_TILES)],
                                         _row_slice(xs_ref, pad_start_ref[e] + r), pad_sem)

        def tile_copy(j):
            return pltpu.make_async_copy(
                zero_ref, xs_ref.at[pl.ds(pl.multiple_of(j * tile_rows, tile_rows), tile_rows)], pad_sem)

        def start_all(e, carry):
            lax.fori_loop(0, pad_n_ref[e], lambda r, c: (pad_copy(e, r).start(), c)[1], 0)
            return carry

        def wait_all(e, carry):
            lax.fori_loop(0, pad_n_ref[e], lambda r, c: (pad_copy(e, r).wait(), c)[1], 0)
            return carry

        lax.fori_loop(0, pad_n_ref.shape[0], start_all, 0)
        lax.fori_loop(n_used_ref[0], n_tiles, lambda j, c: (tile_copy(j).start(), c)[1], 0)
        lax.fori_loop(0, pad_n_ref.shape[0], wait_all, 0)
        lax.fori_loop(n_used_ref[0], n_tiles, lambda j, c: (tile_copy(j).wait(), c)[1], 0)

    @pl.when(i >= 2)
    def _():
        wait_slot(slot)

    _store_token_major(f32_ref, _rms(x1_ref[...], g_ref[...]))
    stage_ref[slot] = f32_ref[...].astype(BF16)

    def issue(t, carry):
        src = stage_ref.at[slot, pl.ds(pl.multiple_of(t * ROW_TILES, ROW_TILES), ROW_TILES)]
        for k in range(2):
            pltpu.make_async_copy(src, _row_slice(xs_ref, pos_ref[0, 0, 2 * t + k]), sems.at[slot]).start()
        return carry

    lax.fori_loop(0, td, issue, 0)

    @pl.when(i == n_steps - 1)
    def _():
        wait_slot(slot)

    @pl.when((i == n_steps - 1) & (i >= 1))
    def _():
        wait_slot(1 - slot)


def _dispatch(x1, g_ffn, pos, pad_start, pad_n, n_used, n_sorted_rows):
    t, d = x1.shape
    td = min(TM_ROW, t)
    pos3 = pos.reshape(t // td, 1, 2 * td)
    smem = pl.BlockSpec(memory_space=pltpu.SMEM)
    return pl.pallas_call(
        _dispatch_kernel,
        grid=(t // td,),
        in_specs=[
            pl.BlockSpec((1, 1, 2 * td), lambda i: (i, 0, 0), memory_space=pltpu.SMEM),
            smem, smem, smem,
            pl.BlockSpec((td, d), lambda i: (i, 0)),
            pl.BlockSpec((1, d), lambda i: (0, 0)),
        ],
        out_specs=pl.BlockSpec(memory_space=pl.ANY),
        out_shape=jax.ShapeDtypeStruct((n_sorted_rows * ROW_TILES, LANES), BF16),
        scratch_shapes=[
            pltpu.VMEM((td * ROW_TILES, LANES), F32),
            pltpu.VMEM((2, td * ROW_TILES, LANES), BF16),
            pltpu.VMEM((TM_EXPERT * ROW_TILES, LANES), BF16),
            pltpu.SemaphoreType.DMA((2,)),
            pltpu.SemaphoreType.DMA(()),
        ],
        compiler_params=_cparams("arbitrary"),
        name="moe_dispatch",
    )(pos3, pad_start, pad_n, n_used, x1, g_ffn)


def _expert_kernel(tile_expert_ref, n_used_ref, xs_ref, wgu_ref, wd_ref, ys_ref, f32_ref, a_ref):
    i = pl.program_id(0)
    tm, d = a_ref.shape
    n_row = d // LANES
    f = wd_ref.shape[0]

    @pl.when(i < n_used_ref[0])
    def _():
        f32_ref[...] = xs_ref[...].astype(F32)
        for c in range(n_row):
            a_ref[:, c * LANES:(c + 1) * LANES] = _load_token_major(f32_ref, tm, n_row, c).astype(BF16)
        gu = jnp.dot(a_ref[...], wgu_ref[...], preferred_element_type=F32)
        hidden = (jax.nn.silu(gu[:, :f]) * gu[:, f:]).astype(BF16)
        _store_token_major(f32_ref, jnp.dot(hidden, wd_ref[...], preferred_element_type=F32))
        ys_ref[...] = f32_ref[...].astype(BF16)

    @pl.when(i >= n_used_ref[0])
    def _():
        ys_ref[...] = jnp.zeros_like(ys_ref)


def _experts(xs, w_gu, w_down, tile_expert, n_used):
    e, d, f2 = w_gu.shape
    n_rows = xs.shape[0] // ROW_TILES
    tm = TM_EXPERT
    n_tiles = n_rows // tm
    return pl.pallas_call(
        _expert_kernel,
        grid_spec=pltpu.PrefetchScalarGridSpec(
            num_scalar_prefetch=2,
            grid=(n_tiles,),
            in_specs=[
                pl.BlockSpec((tm * ROW_TILES, LANES), lambda i, te, nu: (jnp.minimum(i, nu[0] - 1), 0)),
                pl.BlockSpec((None, d, f2), lambda i, te, nu: (te[i], 0, 0)),
                pl.BlockSpec((None, f2 // 2, d), lambda i, te, nu: (te[i], 0, 0)),
            ],
            out_specs=pl.BlockSpec((tm * ROW_TILES, LANES), lambda i, te, nu: (i, 0)),
            scratch_shapes=[pltpu.VMEM((tm * ROW_TILES, LANES), F32), pltpu.VMEM((tm, d), BF16)],
        ),
        out_shape=jax.ShapeDtypeStruct(xs.shape, BF16),
        compiler_params=_cparams("arbitrary"),
        name="moe_experts",
    )(tile_expert, n_used, xs, w_gu, w_down)


def _ple_kernel(pos_ref, pos_next_ref, x1_ref, route_ref, p_ref, gin_ref, wg_ref, wp_ref, gout_ref, ys_ref,
                o_ref, ybuf_ref, f32_ref, x2_ref, sems):
    i = pl.program_id(0)
    n_steps = pl.num_programs(0)
    tm, d = x1_ref.shape
    n_row = d // LANES
    slot = i % 2

    def gather(p_ref_, s):
        def issue(t, carry):
            for k in range(2):
                dst = ybuf_ref.at[s, k, pl.ds(pl.multiple_of(t * ROW_TILES, ROW_TILES), ROW_TILES)]
                pltpu.make_async_copy(_row_slice(ys_ref, p_ref_[0, 0, 2 * t + k]), dst, sems.at[s]).start()
            return carry
        lax.fori_loop(0, tm, issue, 0)

    @pl.when(i == 0)
    def _():
        gather(pos_ref, 0)

    @pl.when(i + 1 < n_steps)
    def _():
        gather(pos_next_ref, 1 - slot)

    for k in range(2):
        pltpu.make_async_copy(ys_ref.at[pl.ds(0, tm * ROW_TILES)], ybuf_ref.at[slot, k], sems.at[slot]).wait()

    w = (route_ref[:, 4:5], route_ref[:, 5:6])
    for k in range(2):
        f32_ref[k] = ybuf_ref[slot, k].astype(F32)
    for c in range(n_row):
        sl = slice(c * LANES, (c + 1) * LANES)
        y0 = _load_token_major(f32_ref.at[0], tm, n_row, c)
        y1 = _load_token_major(f32_ref.at[1], tm, n_row, c)
        x2_ref[:, sl] = x1_ref[:, sl] + w[0] * y0 + w[1] * y1
    x2 = x2_ref[...]
    gate = jax.nn.sigmoid(jnp.dot(_rms(x2, gin_ref[...]).astype(BF16), wg_ref[...],
                                  preferred_element_type=F32))
    e = jnp.dot(p_ref[...].astype(BF16), wp_ref[...], preferred_element_type=F32) * gate
    o_ref[...] = x2 + _rms(e, gout_ref[...])


def _ple(x1, ys, pos, route, p, g_in, w_gate, w_proj, g_out):
    t, d = x1.shape
    tm = min(TM_ROW, t)
    n_steps = t // tm
    pd = p.shape[1]
    pos3 = pos.reshape(n_steps, 1, 2 * tm)
    row_spec = pl.BlockSpec((tm, d), lambda i: (i, 0))
    const = lambda shape: pl.BlockSpec(shape, lambda i: (0,) * len(shape))
    return pl.pallas_call(
        _ple_kernel,
        grid=(n_steps,),
        in_specs=[
            pl.BlockSpec((1, 1, 2 * tm), lambda i: (i, 0, 0), memory_space=pltpu.SMEM),
            pl.BlockSpec((1, 1, 2 * tm), lambda i: (jnp.minimum(i + 1, n_steps - 1), 0, 0),
                         memory_space=pltpu.SMEM),
            row_spec,
            pl.BlockSpec((tm, LANES), lambda i: (i, 0)),
            pl.BlockSpec((tm, pd), lambda i: (i, 0)),
            const((1, d)), const((d, d)), const((pd, d)), const((1, d)),
            pl.BlockSpec(memory_space=pl.ANY),
        ],
        out_specs=row_spec,
        out_shape=jax.ShapeDtypeStruct((t, d), F32),
        scratch_shapes=[
            pltpu.VMEM((2, 2, tm * ROW_TILES, LANES), BF16),
            pltpu.VMEM((2, tm * ROW_TILES, LANES), F32),
            pltpu.VMEM((tm, d), F32),
            pltpu.SemaphoreType.DMA((2,)),
        ],
        compiler_params=_cparams("arbitrary"),
        name="moe_combine_ple",
    )(pos3, pos3, x1, route, p, g_in, w_gate, w_proj, g_out, ys)


def _router_weights(w_group, b_group, w_expert, b_expert):
    d = w_group.shape[0]
    w = jnp.zeros((d, LANES), F32)
    w = w.at[:, :N_EXPERTS].set(w_expert).at[:, ROUTER_GROUP_LANE0:ROUTER_GROUP_LANE0 + N_GROUPS].set(w_group)
    b = jnp.zeros((1, LANES), F32)
    b = b.at[0, :N_EXPERTS].set(b_expert).at[0, ROUTER_GROUP_LANE0:ROUTER_GROUP_LANE0 + N_GROUPS].set(b_group)
    w_hi = w.astype(BF16)
    w_lo = (w - w_hi.astype(F32)).astype(BF16)
    return jnp.concatenate([w_hi, w_hi, w_lo], axis=0), b


def _sorted_layout(route, counts, t):
    tm = TM_EXPERT
    n_tiles = (2 * t + N_EXPERTS * (tm - 1)) // tm + 1
    eid = route[:, 0:2].astype(jnp.int32)
    rank = route[:, 2:4].astype(jnp.int32)
    cnt = counts[0, :N_EXPERTS].astype(jnp.int32)
    padded = ((cnt + tm - 1) // tm) * tm
    ends = jnp.cumsum(padded)
    offs = ends - padded
    pos = offs[eid] + rank
    tile_start = jnp.arange(n_tiles, dtype=jnp.int32) * tm
    n_used = (ends[-1] // tm).astype(jnp.int32)
    probe = jnp.minimum(tile_start, ends[-1] - tm)
    te = jnp.sum((probe[:, None] >= ends[None, :]).astype(jnp.int32), axis=1)
    return dict(pos=pos, tile_expert=te, n_used=n_used.reshape(1), pad_start=offs + cnt,
                pad_n=padded - cnt, n_sorted=n_tiles * tm)


def _moe_and_ple(x, mix_out_kwargs, w_out, g_ffn, router, w_gu, w_down, p_i, ple):
    t = x.shape[0]
    x1, route, counts = _mix_out(x, w_out, g_ffn, *router, **mix_out_kwargs)
    lay = _sorted_layout(route, counts, t)
    xs = _dispatch(x1, g_ffn, lay["pos"], lay["pad_start"], lay["pad_n"], lay["n_used"], lay["n_sorted"])
    ys = _experts(xs, w_gu, w_down, lay["tile_expert"], lay["n_used"])
    return _ple(x1, ys, lay["pos"], route, p_i, *ple)


def kernel(x, p, norm_mix, norm_ffn, sb_w_in, sb_q_norm, sb_k_norm, sb_w_out, sg_w_in, sg_v_norm, sg_w_s, sg_b_s, sg_w_out, moe_w_group, moe_b_group, moe_w_expert, moe_b_expert, moe_w_gate, moe_w_up, moe_w_down, ple_norm_in, ple_w_gate, ple_w_proj, ple_norm_out):
    b, s, d = x.shape
    depth = norm_mix.shape[0]
    heads = d // LANES
    assert d == ROW_TILES * LANES
    t = b * s
    xt = x.reshape(t, d)
    row = lambda v: v.reshape(1, -1)
    for i in range(depth):
        j = i // 2
        router = _router_weights(moe_w_group[i], moe_b_group[i], moe_w_expert[i], moe_b_expert[i])
        w_gu = jnp.concatenate([moe_w_gate[i], moe_w_up[i]], axis=-1).astype(BF16)
        w_down = moe_w_down[i].astype(BF16)
        ple = (row(ple_norm_in[i]), ple_w_gate[i].astype(BF16), ple_w_proj[i].astype(BF16), row(ple_norm_out[i]))
        if i % 2 == 0:
            q_gain = sb_q_norm[j] * (LANES ** -0.5 * LOG2E)
            colgain = jnp.concatenate([jnp.tile(q_gain, heads), jnp.tile(sb_k_norm[j], heads),
                                       jnp.ones((d,), F32)]).reshape(1, 3 * d)
            qkv = _qkv_proj(xt, row(norm_mix[i]), sb_w_in[j].astype(BF16), colgain)
            o = _stick_breaking(qkv.reshape(b, s, 3 * d), heads).reshape(t, d)
            mix = dict(attn_out=o)
            w_out = sb_w_out[j]
        else:
            z = _sg_in_proj(xt, row(norm_mix[i]), sg_w_in[j].astype(BF16), row(sg_v_norm[j]))
            mix = dict(sg=(z, sg_w_s[j], sg_b_s[j].T))
            w_out = sg_w_out[j]
        xt = _moe_and_ple(xt, mix, w_out.astype(BF16), row(norm_ffn[i]), router, w_gu, w_down,
                          p[i].reshape(t, -1), ple)
    return xt.reshape(b, s, d)
```

```python
import functools

import jax
import jax.numpy as jnp
from jax import lax
from jax.experimental import pallas as pl
from jax.experimental.pallas import tpu as pltpu

F32 = jnp.float32
BF16 = jnp.bfloat16

LANES = 128
ROW_TILES = 16
VMEM_LIMIT_BYTES = 56 * 1024 * 1024
EPS = 1e-6
LOG2E = 1.4426950408889634
INV_LN2 = LOG2E
SIGN_BIT = -2 ** 31

N_GROUPS = 4
EXPERTS_PER_GROUP = 8
N_EXPERTS = N_GROUPS * EXPERTS_PER_GROUP
ROUTER_GROUP_LANE0 = N_EXPERTS

TM_PROJ = 512
TM_ROW = 256
TQ = 256
HEADS_PER_STEP = 4
TM_EXPERT = 256


def _cparams(*sem):
    return pltpu.CompilerParams(dimension_semantics=sem, vmem_limit_bytes=VMEM_LIMIT_BYTES)


def _rms(x, g):
    ms = jnp.mean(x * x, axis=-1, keepdims=True)
    return x * lax.rsqrt(ms + EPS) * g


def _qkv_kernel(x_ref, g_ref, w_ref, cg_ref, o_ref, xn_ref, *, n_norm_tiles):
    j = pl.program_id(1)

    @pl.when(j == 0)
    def _():
        xn_ref[...] = _rms(x_ref[...], g_ref[...]).astype(BF16)

    acc = jnp.dot(xn_ref[...], w_ref[...], preferred_element_type=F32)

    @pl.when(j < n_norm_tiles)
    def _():
        for h in range(acc.shape[1] // LANES):
            sl = slice(h * LANES, (h + 1) * LANES)
            a = acc[:, sl]
            ms = jnp.mean(a * a, axis=-1, keepdims=True)
            o_ref[:, sl] = (a * lax.rsqrt(ms + EPS) * cg_ref[:, sl]).astype(BF16)

    @pl.when(j >= n_norm_tiles)
    def _():
        o_ref[...] = acc.astype(BF16)


def _qkv_proj(x, g, w, colgain, *, tn=1024):
    t, d = x.shape
    n = w.shape[1]
    tm = min(TM_PROJ, t)
    tn = min(tn, d)
    return pl.pallas_call(
        functools.partial(_qkv_kernel, n_norm_tiles=2 * d // tn),
        grid=(t // tm, n // tn),
        in_specs=[
            pl.BlockSpec((tm, d), lambda i, j: (i, 0)),
            pl.BlockSpec((1, d), lambda i, j: (0, 0)),
            pl.BlockSpec((d, tn), lambda i, j: (0, j)),
            pl.BlockSpec((1, tn), lambda i, j: (0, j)),
        ],
        out_specs=pl.BlockSpec((tm, tn), lambda i, j: (i, j)),
        out_shape=jax.ShapeDtypeStruct((t, n), BF16),
        scratch_shapes=[pltpu.VMEM((tm, d), BF16)],
        compiler_params=_cparams("parallel", "arbitrary"),
        name="qkv_proj",
    )(x, g, w, colgain)


def _sg_in_kernel(x_ref, g_ref, w_ref, vg_ref, o_ref, xn_ref):
    j = pl.program_id(1)

    @pl.when(j == 0)
    def _():
        xn_ref[...] = _rms(x_ref[...], g_ref[...]).astype(BF16)

    z = jax.nn.gelu(jnp.dot(xn_ref[...], w_ref[...], preferred_element_type=F32))

    @pl.when(j == 0)
    def _():
        o_ref[...] = z.astype(BF16)

    @pl.when(j == 1)
    def _():
        o_ref[...] = _rms(z, vg_ref[...]).astype(BF16)


def _sg_in_proj(x, g, w, v_gain):
    t, d = x.shape
    tm = min(TM_PROJ, t)
    return pl.pallas_call(
        _sg_in_kernel,
        grid=(t // tm, 2),
        in_specs=[
            pl.BlockSpec((tm, d), lambda i, j: (i, 0)),
            pl.BlockSpec((1, d), lambda i, j: (0, 0)),
            pl.BlockSpec((d, d), lambda i, j: (0, j)),
            pl.BlockSpec((1, d), lambda i, j: (0, 0)),
        ],
        out_specs=pl.BlockSpec((tm, d), lambda i, j: (i, j)),
        out_shape=jax.ShapeDtypeStruct((t, 2 * d), BF16),
        scratch_shapes=[pltpu.VMEM((tm, d), BF16)],
        compiler_params=_cparams("parallel", "arbitrary"),
        name="sg_in_proj",
    )(x, g, w, v_gain)


def _attn_kernel(q_ref, k_ref, v_ref, uu_ref, o_ref, acc_ref, carry_ref, z_ref):
    qi = pl.program_id(2)
    tq = q_ref.shape[0]
    tk = uu_ref.shape[1]
    n_heads = q_ref.shape[1] // LANES
    heads = range(n_heads)
    head = lambda ref_or_val, h: ref_or_val[:, h * LANES:(h + 1) * LANES]
    qs = [head(q_ref, h) for h in heads]

    def key_rows(kj):
        return pl.ds(pl.multiple_of(kj * tk, tk), tk)

    def store_logits(kj, slot):
        k_all = k_ref[key_rows(kj), :]
        for h in heads:
            z_ref[slot, h] = lax.dot_general(qs[h], head(k_all, h), (((1,), (1,)), ((), ())),
                                             preferred_element_type=F32)

    def neg_abs(z):
        return lax.bitcast_convert_type(lax.bitcast_convert_type(z, jnp.int32) | SIGN_BIT, F32)

    def tile(kj, slot, causal):
        v_all = v_ref[key_rows(kj), :]
        cs = []
        for h in heads:
            z = z_ref[slot, h]
            sp = jnp.maximum(z, 0.0) + jnp.log(1.0 + jnp.exp2(neg_abs(z))) * INV_LN2
            if causal is not None:
                sp = jnp.where(causal, sp, 0.0)
            cs.append(jnp.dot(sp.astype(BF16), uu_ref[...], preferred_element_type=F32))
        for h in heads:
            w = jnp.exp2(z_ref[slot, h] - cs[h])
            if causal is not None:
                w = jnp.where(causal, w, 0.0)
            pv = jnp.dot(w.astype(BF16), head(v_all, h), preferred_element_type=F32)
            carry = carry_ref[h]
            acc_ref[h] += jnp.exp2(-carry) * pv
            carry_ref[h] = carry + cs[h][:, 0:1]
        store_logits(jnp.maximum(kj - 1, 0), 1 - slot)

    acc_ref[...] = jnp.zeros_like(acc_ref)
    carry_ref[...] = jnp.zeros_like(carry_ref)
    rows = lax.broadcasted_iota(jnp.int32, (tq, tk), 0)
    cols = lax.broadcasted_iota(jnp.int32, (tq, tk), 1)
    store_logits(qi, 0)
    tile(qi, 0, cols < rows)

    def body(it, c):
        tile(qi - 1 - it, (it + 1) % 2, None)
        return c

    lax.fori_loop(0, qi, body, 0)
    for h in heads:
        o_ref[:, h * LANES:(h + 1) * LANES] = acc_ref[h].astype(o_ref.dtype)


def _stick_breaking(qkv, heads):
    b, s, d3 = qkv.shape
    d = d3 // 3
    tq = min(TQ, s)
    tk = tq
    hw = HEADS_PER_STEP * LANES
    groups = heads // HEADS_PER_STEP
    j = lax.broadcasted_iota(jnp.int32, (tk, tk), 0)
    c = lax.broadcasted_iota(jnp.int32, (tk, tk), 1)
    uu = (j >= c).astype(BF16)
    return pl.pallas_call(
        _attn_kernel,
        grid=(b, groups, s // tq),
        in_specs=[
            pl.BlockSpec((None, tq, hw), lambda bi, h, i: (bi, i, h)),
            pl.BlockSpec((None, s, hw), lambda bi, h, i: (bi, 0, groups + h)),
            pl.BlockSpec((None, s, hw), lambda bi, h, i: (bi, 0, 2 * groups + h)),
            pl.BlockSpec((tk, tk), lambda bi, h, i: (0, 0)),
        ],
        out_specs=pl.BlockSpec((None, tq, hw), lambda bi, h, i: (bi, i, h)),
        out_shape=jax.ShapeDtypeStruct((b, s, d), BF16),
        scratch_shapes=[pltpu.VMEM((HEADS_PER_STEP, tq, LANES), F32),
                        pltpu.VMEM((HEADS_PER_STEP, tq, LANES), F32),
                        pltpu.VMEM((2, HEADS_PER_STEP, tq, tk), F32)],
        compiler_params=_cparams("parallel", "parallel", "arbitrary"),
        name="stick_breaking_attention",
    )(qkv, qkv, qkv, uu)


def _split_hi_lo(x):
    hi = x.astype(BF16)
    return hi, (x - hi.astype(F32)).astype(BF16)


def _route_tile(h2, wr_ref, br_ref, ltri_ref, cnt_ref):
    tm = h2.shape[0]
    hi, lo = _split_hi_lo(h2)
    lhs = jnp.concatenate([hi, lo, hi], axis=1)
    logits = jnp.dot(lhs, wr_ref[...], preferred_element_type=F32) + br_ref[...]

    lane = lax.broadcasted_iota(jnp.int32, (tm, LANES), 1)
    neg = jnp.float32(-jnp.inf)
    big = jnp.int32(LANES)

    is_group = (lane >= ROUTER_GROUP_LANE0) & (lane < ROUTER_GROUP_LANE0 + N_GROUPS)
    gl = jnp.where(is_group, logits, neg)
    gmax = jnp.max(gl, axis=-1, keepdims=True)
    g_idx = jnp.min(jnp.where(gl == gmax, lane - ROUTER_GROUP_LANE0, big), axis=-1, keepdims=True)
    g_w = 1.0 / jnp.sum(jnp.where(is_group, jnp.exp(gl - gmax), 0.0), axis=-1, keepdims=True)

    in_group = (lane < N_EXPERTS) & ((lane // EXPERTS_PER_GROUP) == g_idx)
    el = jnp.where(in_group, logits, neg)
    v1 = jnp.max(el, axis=-1, keepdims=True)
    i1 = jnp.min(jnp.where(el == v1, lane, big), axis=-1, keepdims=True)
    el2 = jnp.where(lane == i1, neg, el)
    v2 = jnp.max(el2, axis=-1, keepdims=True)
    i2 = jnp.min(jnp.where(el2 == v2, lane, big), axis=-1, keepdims=True)
    e21 = jnp.exp(v2 - v1)
    den = 1.0 + e21
    w1 = g_w * (1.0 / den)
    w2 = g_w * (e21 / den)

    oh1 = lane == i1
    oh2 = lane == i2
    onehot = (oh1 | oh2).astype(BF16)
    ahead = jnp.dot(ltri_ref[...], onehot, preferred_element_type=F32) + cnt_ref[...]
    r1 = jnp.sum(jnp.where(oh1, ahead, 0.0), axis=-1, keepdims=True)
    r2 = jnp.sum(jnp.where(oh2, ahead, 0.0), axis=-1, keepdims=True)
    cnt_ref[...] += jnp.sum(onehot.astype(F32), axis=0, keepdims=True)

    rec = jnp.where(lane == 0, i1.astype(F32), 0.0)
    rec = jnp.where(lane == 1, i2.astype(F32), rec)
    rec = jnp.where(lane == 2, r1, rec)
    rec = jnp.where(lane == 3, r2, rec)
    rec = jnp.where(lane == 4, w1, rec)
    rec = jnp.where(lane == 5, w2, rec)
    return rec


def _store_token_major(ref, val):
    tm = val.shape[0]
    n = val.shape[1] // LANES
    for c in range(n):
        ref[pl.ds(c, tm, stride=n), :] = val[:, c * LANES:(c + 1) * LANES]


def _load_token_major(ref, tm, n, c):
    return ref[pl.ds(c, tm, stride=n), :]


def _mix_out_epilogue(a, x_ref, wo_ref, gf_ref, wr_ref, br_ref, x1_ref, route_ref, cnt_out_ref,
                      ltri_ref, cnt_ref):
    tm = a.shape[0]

    @pl.when(pl.program_id(0) == 0)
    def _():
        r = lax.broadcasted_iota(jnp.int32, (tm, tm), 0)
        c = lax.broadcasted_iota(jnp.int32, (tm, tm), 1)
        ltri_ref[...] = (c < r).astype(BF16)
        cnt_ref[...] = jnp.zeros_like(cnt_ref)

    x1 = x_ref[...] + jnp.dot(a, wo_ref[...], preferred_element_type=F32)
    x1_ref[...] = x1
    route_ref[...] = _route_tile(_rms(x1, gf_ref[...]), wr_ref, br_ref, ltri_ref, cnt_ref)
    cnt_out_ref[...] = cnt_ref[...]


def _attn_out_kernel(a_ref, x_ref, wo_ref, gf_ref, wr_ref, br_ref,
                     x1_ref, route_ref, cnt_out_ref, ltri_ref, cnt_ref):
    _mix_out_epilogue(a_ref[...], x_ref, wo_ref, gf_ref, wr_ref, br_ref,
                      x1_ref, route_ref, cnt_out_ref, ltri_ref, cnt_ref)


def _sg_out_kernel(u_ref, vn_ref, ws_ref, bs_ref, x_ref, wo_ref, gf_ref, wr_ref, br_ref,
                   x1_ref, route_ref, cnt_out_ref, ltri_ref, cnt_ref, a_ref):
    tm, d = u_ref.shape
    r = lax.broadcasted_iota(jnp.int32, (LANES, LANES), 0)
    c = lax.broadcasted_iota(jnp.int32, (LANES, LANES), 1)
    keep = c <= r
    for g in range(d // LANES):
        sl = slice(g * LANES, (g + 1) * LANES)
        wc = jnp.where(keep, ws_ref[g], 0.0).astype(BF16)
        bias = bs_ref[:, g:g + 1]
        for ch in range(tm // LANES):
            rs = slice(ch * LANES, (ch + 1) * LANES)
            mixed = jnp.dot(wc, vn_ref[rs, sl], preferred_element_type=F32) + bias
            a_ref[rs, sl] = (u_ref[rs, sl].astype(F32) * mixed).astype(BF16)
    _mix_out_epilogue(a_ref[...], x_ref, wo_ref, gf_ref, wr_ref, br_ref,
                      x1_ref, route_ref, cnt_out_ref, ltri_ref, cnt_ref)


def _mix_out(x, w_out, g_ffn, w_router, b_router, *, attn_out=None, sg=None):
    t, d = x.shape
    tm = min(TM_ROW, t)
    row_spec = pl.BlockSpec((tm, d), lambda i: (i, 0))
    const = lambda shape: pl.BlockSpec(shape, lambda i: (0,) * len(shape))
    common_in = [row_spec, const((d, d)), const((1, d)), const(w_router.shape), const((1, LANES))]
    common_args = (x, w_out, g_ffn, w_router, b_router)
    out_specs = [row_spec, pl.BlockSpec((tm, LANES), lambda i: (i, 0)), const((1, LANES))]
    out_shape = [
        jax.ShapeDtypeStruct((t, d), F32),
        jax.ShapeDtypeStruct((t, LANES), F32),
        jax.ShapeDtypeStruct((1, LANES), F32),
    ]
    scratch = [pltpu.VMEM((tm, tm), BF16), pltpu.VMEM((1, LANES), F32)]
    if attn_out is not None:
        return pl.pallas_call(
            _attn_out_kernel,
            grid=(t // tm,),
            in_specs=[row_spec] + common_in,
            out_specs=out_specs, out_shape=out_shape, scratch_shapes=scratch,
            compiler_params=_cparams("arbitrary"),
            name="attn_out_router",
        )(attn_out, *common_args)
    z, w_s, b_s_t = sg
    return pl.pallas_call(
        _sg_out_kernel,
        grid=(t // tm,),
        in_specs=[
            pl.BlockSpec((tm, d), lambda i: (i, 0)),
            pl.BlockSpec((tm, d), lambda i: (i, 1)),
            const(w_s.shape),
            const(b_s_t.shape),
        ] + common_in,
        out_specs=out_specs, out_shape=out_shape,
        scratch_shapes=scratch + [pltpu.VMEM((tm, d), BF16)],
        compiler_params=_cparams("arbitrary"),
        name="sg_out_router",
    )(z, z, w_s, b_s_t, *common_args)


def _row_slice(ref, row):
    return ref.at[pl.ds(pl.multiple_of(row * ROW_TILES, ROW_TILES), ROW_TILES)]


def _for_each_token(n, fn, unroll=8):
    def group(o, carry):
        for u in range(unroll):
            fn(o * unroll + u)
        return carry
    lax.fori_loop(0, n // unroll, group, 0)


def _dispatch_kernel(pos_ref, pad_start_ref, pad_n_ref, n_used_ref, x1_ref, g_ref, xs_ref,
                     f32_ref, stage_ref, zero_ref, sems, pad_sem):
    i = pl.program_id(0)
    n_steps = pl.num_programs(0)
    td = x1_ref.shape[0]
    slot = i % 2

    def wait_slot(s):
        for _ in range(2):
            pltpu.make_async_copy(stage_ref.at[s], xs_ref.at[pl.ds(0, td * ROW_TILES)], sems.at[s]).wait()

    @pl.when(i == 0)
    def _():
        zero_ref[...] = jnp.zeros_like(zero_ref)
        tile_rows = zero_ref.shape[0]
        n_tiles = xs_ref.shape[0] // tile_rows

        def pad_copy(e, r):
            return pltpu.make_async_copy(zero_ref.at[pl.ds(0, ROW_TILES)],
                                         _row_slice(xs_ref, pad_start_ref[e] + r), pad_sem)

        def tile_copy(j):
            return pltpu.make_async_copy(
                zero_ref, xs_ref.at[pl.ds(pl.multiple_of(j * tile_rows, tile_rows), tile_rows)], pad_sem)

        def start_all(e, carry):
            lax.fori_loop(0, pad_n_ref[e], lambda r, c: (pad_copy(e, r).start(), c)[1], 0)
            return carry

        def wait_all(e, carry):
            lax.fori_loop(0, pad_n_ref[e], lambda r, c: (pad_copy(e, r).wait(), c)[1], 0)
            return carry

        lax.fori_loop(0, pad_n_ref.shape[0], start_all, 0)
        lax.fori_loop(n_used_ref[0], n_tiles, lambda j, c: (tile_copy(j).start(), c)[1], 0)
        lax.fori_loop(0, pad_n_ref.shape[0], wait_all, 0)
        lax.fori_loop(n_used_ref[0], n_tiles, lambda j, c: (tile_copy(j).wait(), c)[1], 0)

    @pl.when(i >= 2)
    def _():
        wait_slot(slot)

    _store_token_major(f32_ref, _rms(x1_ref[...], g_ref[...]))
    stage_ref[slot] = f32_ref[...].astype(BF16)

    def issue(t):
        src = stage_ref.at[slot, pl.ds(pl.multiple_of(t * ROW_TILES, ROW_TILES), ROW_TILES)]
        for k in range(2):
            pltpu.make_async_copy(src, _row_slice(xs_ref, pos_ref[0, 0, 2 * t + k]), sems.at[slot]).start()

    _for_each_token(td, issue)

    @pl.when(i == n_steps - 1)
    def _():
        wait_slot(slot)

    @pl.when((i == n_steps - 1) & (i >= 1))
    def _():
        wait_slot(1 - slot)


def _dispatch(x1, g_ffn, pos, pad_start, pad_n, n_used, n_sorted_rows):
    t, d = x1.shape
    td = min(TM_ROW, t)
    pos3 = pos.reshape(t // td, 1, 2 * td)
    smem = pl.BlockSpec(memory_space=pltpu.SMEM)
    return pl.pallas_call(
        _dispatch_kernel,
        grid=(t // td,),
        in_specs=[
            pl.BlockSpec((1, 1, 2 * td), lambda i: (i, 0, 0), memory_space=pltpu.SMEM),
            smem, smem, smem,
            pl.BlockSpec((td, d), lambda i: (i, 0)),
            pl.BlockSpec((1, d), lambda i: (0, 0)),
        ],
        out_specs=pl.BlockSpec(memory_space=pl.ANY),
        out_shape=jax.ShapeDtypeStruct((n_sorted_rows * ROW_TILES, LANES), BF16),
        scratch_shapes=[
            pltpu.VMEM((td * ROW_TILES, LANES), F32),
            pltpu.VMEM((2, td * ROW_TILES, LANES), BF16),
            pltpu.VMEM((TM_EXPERT * ROW_TILES, LANES), BF16),
            pltpu.SemaphoreType.DMA((2,)),
            pltpu.SemaphoreType.DMA(()),
        ],
        compiler_params=_cparams("arbitrary"),
        name="moe_dispatch",
    )(pos3, pad_start, pad_n, n_used, x1, g_ffn)


def _expert_kernel(tile_expert_ref, n_used_ref, xs_ref, wg_ref, wu_ref, wd_ref, ys_ref,
                   f32_ref, a_ref, wgu_bf_ref, wd_bf_ref):
    i = pl.program_id(0)
    tm, d = a_ref.shape
    n_row = d // LANES
    f = wd_ref.shape[0]

    @pl.when((i == 0) | (tile_expert_ref[i] != tile_expert_ref[jnp.maximum(i - 1, 0)]))
    def _():
        wgu_bf_ref[:, :f] = wg_ref[...].astype(BF16)
        wgu_bf_ref[:, f:] = wu_ref[...].astype(BF16)
        wd_bf_ref[...] = wd_ref[...].astype(BF16)

    @pl.when(i < n_used_ref[0])
    def _():
        f32_ref[...] = xs_ref[...].astype(F32)
        for c in range(n_row):
            a_ref[:, c * LANES:(c + 1) * LANES] = _load_token_major(f32_ref, tm, n_row, c).astype(BF16)
        gu = jnp.dot(a_ref[...], wgu_bf_ref[...], preferred_element_type=F32)
        hidden = (jax.nn.silu(gu[:, :f]) * gu[:, f:]).astype(BF16)
        _store_token_major(f32_ref, jnp.dot(hidden, wd_bf_ref[...], preferred_element_type=F32))
        ys_ref[...] = f32_ref[...].astype(BF16)

    @pl.when(i >= n_used_ref[0])
    def _():
        ys_ref[...] = jnp.zeros_like(ys_ref)


def _experts(xs, layer, w_gate, w_up, w_down, tile_expert, n_used):
    _, e, d, f = w_gate.shape
    n_rows = xs.shape[0] // ROW_TILES
    tm = TM_EXPERT
    n_tiles = n_rows // tm
    return pl.pallas_call(
        _expert_kernel,
        grid_spec=pltpu.PrefetchScalarGridSpec(
            num_scalar_prefetch=2,
            grid=(n_tiles,),
            in_specs=[
                pl.BlockSpec((tm * ROW_TILES, LANES), lambda i, te, nu: (jnp.minimum(i, nu[0] - 1), 0)),
                pl.BlockSpec((None, None, d, f), lambda i, te, nu: (layer, te[i], 0, 0)),
                pl.BlockSpec((None, None, d, f), lambda i, te, nu: (layer, te[i], 0, 0)),
                pl.BlockSpec((None, None, f, d), lambda i, te, nu: (layer, te[i], 0, 0)),
            ],
            out_specs=pl.BlockSpec((tm * ROW_TILES, LANES), lambda i, te, nu: (i, 0)),
            scratch_shapes=[pltpu.VMEM((tm * ROW_TILES, LANES), F32), pltpu.VMEM((tm, d), BF16),
                            pltpu.VMEM((d, 2 * f), BF16), pltpu.VMEM((f, d), BF16)],
        ),
        out_shape=jax.ShapeDtypeStruct(xs.shape, BF16),
        compiler_params=_cparams("arbitrary"),
        name="moe_experts",
    )(tile_expert, n_used, xs, w_gate, w_up, w_down)


def _ple_kernel(pos_ref, pos_next_ref, x1_ref, route_ref, p_ref, gin_ref, wg_ref, wp_ref, gout_ref, ys_ref,
                o_ref, ybuf_ref, f32_ref, x2_ref, sems):
    i = pl.program_id(0)
    n_steps = pl.num_programs(0)
    tm, d = x1_ref.shape
    n_row = d // LANES
    slot = i % 2

    def gather(p_ref_, s):
        def issue(t):
            for k in range(2):
                dst = ybuf_ref.at[s, k, pl.ds(pl.multiple_of(t * ROW_TILES, ROW_TILES), ROW_TILES)]
                pltpu.make_async_copy(_row_slice(ys_ref, p_ref_[0, 0, 2 * t + k]), dst, sems.at[s]).start()
        _for_each_token(tm, issue)

    @pl.when(i == 0)
    def _():
        gather(pos_ref, 0)

    @pl.when(i + 1 < n_steps)
    def _():
        gather(pos_next_ref, 1 - slot)

    for k in range(2):
        pltpu.make_async_copy(ys_ref.at[pl.ds(0, tm * ROW_TILES)], ybuf_ref.at[slot, k], sems.at[slot]).wait()

    w = (route_ref[:, 4:5], route_ref[:, 5:6])
    for k in range(2):
        f32_ref[k] = ybuf_ref[slot, k].astype(F32)
    for c in range(n_row):
        sl = slice(c * LANES, (c + 1) * LANES)
        y0 = _load_token_major(f32_ref.at[0], tm, n_row, c)
        y1 = _load_token_major(f32_ref.at[1], tm, n_row, c)
        x2_ref[:, sl] = x1_ref[:, sl] + w[0] * y0 + w[1] * y1
    x2 = x2_ref[...]
    gate = jax.nn.sigmoid(jnp.dot(_rms(x2, gin_ref[...]).astype(BF16), wg_ref[...],
                                  preferred_element_type=F32))
    e = jnp.dot(p_ref[...].astype(BF16), wp_ref[...], preferred_element_type=F32) * gate
    o_ref[...] = x2 + _rms(e, gout_ref[...])


def _ple(x1, ys, pos, route, p, g_in, w_gate, w_proj, g_out):
    t, d = x1.shape
    tm = min(TM_ROW, t)
    n_steps = t // tm
    pd = p.shape[1]
    pos3 = pos.reshape(n_steps, 1, 2 * tm)
    row_spec = pl.BlockSpec((tm, d), lambda i: (i, 0))
    const = lambda shape: pl.BlockSpec(shape, lambda i: (0,) * len(shape))
    return pl.pallas_call(
        _ple_kernel,
        grid=(n_steps,),
        in_specs=[
            pl.BlockSpec((1, 1, 2 * tm), lambda i: (i, 0, 0), memory_space=pltpu.SMEM),
            pl.BlockSpec((1, 1, 2 * tm), lambda i: (jnp.minimum(i + 1, n_steps - 1), 0, 0),
                         memory_space=pltpu.SMEM),
            row_spec,
            pl.BlockSpec((tm, LANES), lambda i: (i, 0)),
            pl.BlockSpec((tm, pd), lambda i: (i, 0)),
            const((1, d)), const((d, d)), const((pd, d)), const((1, d)),
            pl.BlockSpec(memory_space=pl.ANY),
        ],
        out_specs=row_spec,
        out_shape=jax.ShapeDtypeStruct((t, d), F32),
        scratch_shapes=[
            pltpu.VMEM((2, 2, tm * ROW_TILES, LANES), BF16),
            pltpu.VMEM((2, tm * ROW_TILES, LANES), F32),
            pltpu.VMEM((tm, d), F32),
            pltpu.SemaphoreType.DMA((2,)),
        ],
        compiler_params=_cparams("arbitrary"),
        name="moe_combine_ple",
    )(pos3, pos3, x1, route, p, g_in, w_gate, w_proj, g_out, ys)


def _router_weights(w_group, b_group, w_expert, b_expert):
    d = w_group.shape[0]
    w = jnp.zeros((d, LANES), F32)
    w = w.at[:, :N_EXPERTS].set(w_expert).at[:, ROUTER_GROUP_LANE0:ROUTER_GROUP_LANE0 + N_GROUPS].set(w_group)
    b = jnp.zeros((1, LANES), F32)
    b = b.at[0, :N_EXPERTS].set(b_expert).at[0, ROUTER_GROUP_LANE0:ROUTER_GROUP_LANE0 + N_GROUPS].set(b_group)
    w_hi = w.astype(BF16)
    w_lo = (w - w_hi.astype(F32)).astype(BF16)
    return jnp.concatenate([w_hi, w_hi, w_lo], axis=0), b


def _sorted_layout(route, counts, t):
    tm = TM_EXPERT
    n_tiles = (2 * t + N_EXPERTS * (tm - 1)) // tm + 1
    eid = route[:, 0:2].astype(jnp.int32)
    rank = route[:, 2:4].astype(jnp.int32)
    cnt = counts[0, :N_EXPERTS].astype(jnp.int32)
    padded = ((cnt + tm - 1) // tm) * tm
    ends = jnp.cumsum(padded)
    offs = ends - padded
    pos = offs[eid] + rank
    tile_start = jnp.arange(n_tiles, dtype=jnp.int32) * tm
    n_used = (ends[-1] // tm).astype(jnp.int32)
    probe = jnp.minimum(tile_start, ends[-1] - tm)
    te = jnp.sum((probe[:, None] >= ends[None, :]).astype(jnp.int32), axis=1)
    return dict(pos=pos, tile_expert=te, n_used=n_used.reshape(1), pad_start=offs + cnt,
                pad_n=padded - cnt, n_sorted=n_tiles * tm)


def _moe_and_ple(x, mix_out_kwargs, w_out, g_ffn, router, expert_w, p_i, ple):
    t = x.shape[0]
    x1, route, counts = _mix_out(x, w_out, g_ffn, *router, **mix_out_kwargs)
    lay = _sorted_layout(route, counts, t)
    xs = _dispatch(x1, g_ffn, lay["pos"], lay["pad_start"], lay["pad_n"], lay["n_used"], lay["n_sorted"])
    ys = _experts(xs, *expert_w, lay["tile_expert"], lay["n_used"])
    return _ple(x1, ys, lay["pos"], route, p_i, *ple)


def kernel(x, p, norm_mix, norm_ffn, sb_w_in, sb_q_norm, sb_k_norm, sb_w_out, sg_w_in, sg_v_norm, sg_w_s, sg_b_s, sg_w_out, moe_w_group, moe_b_group, moe_w_expert, moe_b_expert, moe_w_gate, moe_w_up, moe_w_down, ple_norm_in, ple_w_gate, ple_w_proj, ple_norm_out):
    b, s, d = x.shape
    depth = norm_mix.shape[0]
    heads = d // LANES
    assert d == ROW_TILES * LANES
    t = b * s
    xt = x.reshape(t, d)
    row = lambda v: v.reshape(1, -1)
    for i in range(depth):
        j = i // 2
        router = _router_weights(moe_w_group[i], moe_b_group[i], moe_w_expert[i], moe_b_expert[i])
        expert_w = (i, moe_w_gate, moe_w_up, moe_w_down)
        ple = (row(ple_norm_in[i]), ple_w_gate[i].astype(BF16), ple_w_proj[i].astype(BF16), row(ple_norm_out[i]))
        if i % 2 == 0:
            q_gain = sb_q_norm[j] * (LANES ** -0.5 * LOG2E)
            colgain = jnp.concatenate([jnp.tile(q_gain, heads), jnp.tile(sb_k_norm[j], heads),
                                       jnp.ones((d,), F32)]).reshape(1, 3 * d)
            qkv = _qkv_proj(xt, row(norm_mix[i]), sb_w_in[j].astype(BF16), colgain)
            o = _stick_breaking(qkv.reshape(b, s, 3 * d), heads).reshape(t, d)
            mix = dict(attn_out=o)
            w_out = sb_w_out[j]
        else:
            z = _sg_in_proj(xt, row(norm_mix[i]), sg_w_in[j].astype(BF16), row(sg_v_norm[j]))
            mix = dict(sg=(z, sg_w_s[j], sg_b_s[j].T))
            w_out = sg_w_out[j]
        xt = _moe_and_ple(xt, mix, w_out.astype(BF16), row(norm_ffn[i]), router, expert_w,
                          p[i].reshape(t, -1), ple)
    return xt.reshape(b, s, d)
```

```python
import functools

import jax
import jax.numpy as jnp
from jax import lax
from jax.experimental import pallas as pl
from jax.experimental.pallas import tpu as pltpu

F32 = jnp.float32
BF16 = jnp.bfloat16

LANES = 128
ROW_TILES = 16
HALF_TILES = ROW_TILES // 2
VMEM_LIMIT_BYTES = 56 * 1024 * 1024
EPS = 1e-6
LOG2E = 1.4426950408889634
INV_LN2 = LOG2E
SIGN_BIT = -2 ** 31

N_GROUPS = 4
EXPERTS_PER_GROUP = 8
N_EXPERTS = N_GROUPS * EXPERTS_PER_GROUP
ROUTER_GROUP_LANE0 = N_EXPERTS

TM_PROJ = 512
TM_ROW = 256
TQ = 256
TK = 256
HEADS_PER_STEP = 4
TM_EXPERT = 256


def _cparams(*sem):
    return pltpu.CompilerParams(dimension_semantics=sem, vmem_limit_bytes=VMEM_LIMIT_BYTES)


def _rms(x, g):
    ms = jnp.mean(x * x, axis=-1, keepdims=True)
    return x * lax.rsqrt(ms + EPS) * g


def _qkv_kernel(x_ref, g_ref, w_ref, cg_ref, o_ref, xn_ref, *, n_norm_tiles):
    j = pl.program_id(1)

    @pl.when(j == 0)
    def _():
        xn_ref[...] = _rms(x_ref[...], g_ref[...]).astype(BF16)

    is_norm = j < n_norm_tiles
    acc = jnp.dot(xn_ref[...], w_ref[...], preferred_element_type=F32)
    for h in range(acc.shape[1] // LANES):
        sl = slice(h * LANES, (h + 1) * LANES)
        a = acc[:, sl]
        ms = jnp.mean(a * a, axis=-1, keepdims=True)
        scale = jnp.where(is_norm, lax.rsqrt(ms + EPS), 1.0)
        o_ref[:, sl] = (a * scale * cg_ref[:, sl]).astype(BF16)


def _qkv_proj(x, g, w, colgain, *, tn=1024):
    t, d = x.shape
    n = w.shape[1]
    tm = min(TM_PROJ, t)
    tn = min(tn, d)
    return pl.pallas_call(
        functools.partial(_qkv_kernel, n_norm_tiles=2 * d // tn),
        grid=(t // tm, n // tn),
        in_specs=[
            pl.BlockSpec((tm, d), lambda i, j: (i, 0)),
            pl.BlockSpec((1, d), lambda i, j: (0, 0)),
            pl.BlockSpec((d, tn), lambda i, j: (0, j)),
            pl.BlockSpec((1, tn), lambda i, j: (0, j)),
        ],
        out_specs=pl.BlockSpec((tm, tn), lambda i, j: (i, j)),
        out_shape=jax.ShapeDtypeStruct((t, n), BF16),
        scratch_shapes=[pltpu.VMEM((tm, d), BF16)],
        compiler_params=_cparams("parallel", "arbitrary"),
        name="qkv_proj",
    )(x, g, w, colgain)


def _sg_in_kernel(x_ref, g_ref, w_ref, vg_ref, o_ref, xn_ref):
    j = pl.program_id(1)

    @pl.when(j == 0)
    def _():
        xn_ref[...] = _rms(x_ref[...], g_ref[...]).astype(BF16)

    z = jax.nn.gelu(jnp.dot(xn_ref[...], w_ref[...], preferred_element_type=F32))
    is_v = j == 1
    ms = jnp.mean(z * z, axis=-1, keepdims=True)
    scale = jnp.where(is_v, lax.rsqrt(ms + EPS), 1.0)
    gain = jnp.where(is_v, vg_ref[...], 1.0)
    o_ref[...] = (z * scale * gain).astype(BF16)


def _sg_in_proj(x, g, w, v_gain):
    t, d = x.shape
    tm = min(TM_PROJ, t)
    return pl.pallas_call(
        _sg_in_kernel,
        grid=(t // tm, 2),
        in_specs=[
            pl.BlockSpec((tm, d), lambda i, j: (i, 0)),
            pl.BlockSpec((1, d), lambda i, j: (0, 0)),
            pl.BlockSpec((d, d), lambda i, j: (0, j)),
            pl.BlockSpec((1, d), lambda i, j: (0, 0)),
        ],
        out_specs=pl.BlockSpec((tm, d), lambda i, j: (i, j)),
        out_shape=jax.ShapeDtypeStruct((t, 2 * d), BF16),
        scratch_shapes=[pltpu.VMEM((tm, d), BF16)],
        compiler_params=_cparams("parallel", "arbitrary"),
        name="sg_in_proj",
    )(x, g, w, v_gain)


def _attn_kernel(q_ref, k_ref, v_ref, uu_ref, o_ref, acc_ref, carry_ref, z_ref):
    qi = pl.program_id(2)
    tq = q_ref.shape[0]
    tk = uu_ref.shape[1]
    n_heads = q_ref.shape[1] // LANES
    heads = range(n_heads)
    head = lambda ref_or_val, h: ref_or_val[:, h * LANES:(h + 1) * LANES]
    qs = [head(q_ref, h) for h in heads]

    def key_rows(kj):
        return pl.ds(pl.multiple_of(kj * tk, tk), tk)

    def store_logits(kj, slot):
        k_all = k_ref[key_rows(kj), :]
        for h in heads:
            z_ref[slot, h] = lax.dot_general(qs[h], head(k_all, h), (((1,), (1,)), ((), ())),
                                             preferred_element_type=F32)

    def neg_abs(z):
        return lax.bitcast_convert_type(lax.bitcast_convert_type(z, jnp.int32) | SIGN_BIT, F32)

    def tile(kj, slot, causal):
        v_all = v_ref[key_rows(kj), :]
        cs = []
        for h in heads:
            z = z_ref[slot, h]
            sp = jnp.maximum(z, 0.0) + jnp.log(1.0 + jnp.exp2(neg_abs(z))) * INV_LN2
            if causal is not None:
                sp = jnp.where(causal, sp, 0.0)
            cs.append(jnp.dot(sp.astype(BF16), uu_ref[...], preferred_element_type=F32))
        for h in heads:
            w = jnp.exp2(z_ref[slot, h] - cs[h])
            if causal is not None:
                w = jnp.where(causal, w, 0.0)
            pv = jnp.dot(w.astype(BF16), head(v_all, h), preferred_element_type=F32)
            carry = carry_ref[h]
            acc_ref[h] += jnp.exp2(-carry) * pv
            carry_ref[h] = carry + cs[h][:, 0:1]
        store_logits(jnp.maximum(kj - 1, 0), 1 - slot)

    acc_ref[...] = jnp.zeros_like(acc_ref)
    carry_ref[...] = jnp.zeros_like(carry_ref)
    rows = lax.broadcasted_iota(jnp.int32, (tq, tk), 0)
    cols = lax.broadcasted_iota(jnp.int32, (tq, tk), 1)
    ratio = tq // tk
    first = qi * ratio
    store_logits(first + ratio - 1, 0)
    for n in range(ratio):
        m = ratio - 1 - n
        tile(first + m, n % 2, cols + m * tk < rows)

    def body(it, c):
        tile(first - 1 - it, (ratio + it) % 2, None)
        return c

    lax.fori_loop(0, first, body, 0)
    for h in heads:
        o_ref[:, h * LANES:(h + 1) * LANES] = acc_ref[h].astype(o_ref.dtype)


def _stick_breaking(qkv, heads):
    b, s, d3 = qkv.shape
    d = d3 // 3
    tq = min(TQ, s)
    tk = min(TK, s)
    hw = HEADS_PER_STEP * LANES
    groups = heads // HEADS_PER_STEP
    j = lax.broadcasted_iota(jnp.int32, (tk, tk), 0)
    c = lax.broadcasted_iota(jnp.int32, (tk, tk), 1)
    uu = (j >= c).astype(BF16)
    return pl.pallas_call(
        _attn_kernel,
        grid=(b, groups, s // tq),
        in_specs=[
            pl.BlockSpec((None, tq, hw), lambda bi, h, i: (bi, i, h)),
            pl.BlockSpec((None, s, hw), lambda bi, h, i: (bi, 0, groups + h)),
            pl.BlockSpec((None, s, hw), lambda bi, h, i: (bi, 0, 2 * groups + h)),
            pl.BlockSpec((tk, tk), lambda bi, h, i: (0, 0)),
        ],
        out_specs=pl.BlockSpec((None, tq, hw), lambda bi, h, i: (bi, i, h)),
        out_shape=jax.ShapeDtypeStruct((b, s, d), BF16),
        scratch_shapes=[pltpu.VMEM((HEADS_PER_STEP, tq, LANES), F32),
                        pltpu.VMEM((HEADS_PER_STEP, tq, LANES), F32),
                        pltpu.VMEM((2, HEADS_PER_STEP, tq, tk), F32)],
        compiler_params=_cparams("parallel", "parallel", "arbitrary"),
        name="stick_breaking_attention",
    )(qkv, qkv, qkv, uu)


def _split_hi_lo(x):
    hi = x.astype(BF16)
    return hi, (x - hi.astype(F32)).astype(BF16)


def _route_tile(h2, wr_ref, br_ref, ltri_ref, cnt_ref):
    tm = h2.shape[0]
    hi, lo = _split_hi_lo(h2)
    lhs = jnp.concatenate([hi, lo, hi], axis=1)
    logits = jnp.dot(lhs, wr_ref[...], preferred_element_type=F32) + br_ref[...]

    lane = lax.broadcasted_iota(jnp.int32, (tm, LANES), 1)
    neg = jnp.float32(-jnp.inf)
    big = jnp.int32(LANES)

    is_group = (lane >= ROUTER_GROUP_LANE0) & (lane < ROUTER_GROUP_LANE0 + N_GROUPS)
    gl = jnp.where(is_group, logits, neg)
    gmax = jnp.max(gl, axis=-1, keepdims=True)
    g_idx = jnp.min(jnp.where(gl == gmax, lane - ROUTER_GROUP_LANE0, big), axis=-1, keepdims=True)
    g_w = 1.0 / jnp.sum(jnp.where(is_group, jnp.exp(gl - gmax), 0.0), axis=-1, keepdims=True)

    in_group = (lane < N_EXPERTS) & ((lane // EXPERTS_PER_GROUP) == g_idx)
    el = jnp.where(in_group, logits, neg)
    v1 = jnp.max(el, axis=-1, keepdims=True)
    i1 = jnp.min(jnp.where(el == v1, lane, big), axis=-1, keepdims=True)
    el2 = jnp.where(lane == i1, neg, el)
    v2 = jnp.max(el2, axis=-1, keepdims=True)
    i2 = jnp.min(jnp.where(el2 == v2, lane, big), axis=-1, keepdims=True)
    e21 = jnp.exp(v2 - v1)
    den = 1.0 + e21
    w1 = g_w * (1.0 / den)
    w2 = g_w * (e21 / den)

    oh1 = lane == i1
    oh2 = lane == i2
    onehot = (oh1 | oh2).astype(BF16)
    ahead = jnp.dot(ltri_ref[...], onehot, preferred_element_type=F32) + cnt_ref[...]
    r1 = jnp.sum(jnp.where(oh1, ahead, 0.0), axis=-1, keepdims=True)
    r2 = jnp.sum(jnp.where(oh2, ahead, 0.0), axis=-1, keepdims=True)
    cnt_ref[...] += jnp.sum(onehot.astype(F32), axis=0, keepdims=True)

    rec = jnp.where(lane == 0, i1.astype(F32), 0.0)
    rec = jnp.where(lane == 1, i2.astype(F32), rec)
    rec = jnp.where(lane == 2, r1, rec)
    rec = jnp.where(lane == 3, r2, rec)
    rec = jnp.where(lane == 4, w1, rec)
    rec = jnp.where(lane == 5, w2, rec)
    return rec


def _store_chunks(ref, val):
    tm = val.shape[0]
    for c in range(ROW_TILES):
        ref.at[c // HALF_TILES][pl.ds(c % HALF_TILES, tm, stride=HALF_TILES), :] = val[:, c * LANES:(c + 1) * LANES]


def _load_chunk(ref, tm, c):
    return ref.at[c // HALF_TILES][pl.ds(c % HALF_TILES, tm, stride=HALF_TILES), :]


def _halves_from_token_major(ref, val):
    tm = val.shape[0] // ROW_TILES
    v4 = val.reshape(tm, 2, HALF_TILES, LANES)
    for j in range(2):
        ref[j] = v4[:, j].reshape(tm * HALF_TILES, LANES)


def _token_major_from_halves(ref):
    tm = ref.shape[1] // HALF_TILES
    parts = [ref[j].reshape(tm, 1, HALF_TILES, LANES) for j in range(2)]
    return jnp.concatenate(parts, axis=1).reshape(tm * ROW_TILES, LANES)


def _mix_out_epilogue(a, x_ref, wo_ref, gf_ref, wr_ref, br_ref, x1_ref, route_ref, cnt_out_ref,
                      ltri_ref, cnt_ref):
    tm = a.shape[0]

    @pl.when(pl.program_id(0) == 0)
    def _():
        r = lax.broadcasted_iota(jnp.int32, (tm, tm), 0)
        c = lax.broadcasted_iota(jnp.int32, (tm, tm), 1)
        ltri_ref[...] = (c < r).astype(BF16)
        cnt_ref[...] = jnp.zeros_like(cnt_ref)

    x1 = x_ref[...] + jnp.dot(a, wo_ref[...], preferred_element_type=F32)
    x1_ref[...] = x1
    route_ref[...] = _route_tile(_rms(x1, gf_ref[...]), wr_ref, br_ref, ltri_ref, cnt_ref)
    cnt_out_ref[...] = cnt_ref[...]


def _attn_out_kernel(a_ref, x_ref, wo_ref, gf_ref, wr_ref, br_ref,
                     x1_ref, route_ref, cnt_out_ref, ltri_ref, cnt_ref):
    _mix_out_epilogue(a_ref[...], x_ref, wo_ref, gf_ref, wr_ref, br_ref,
                      x1_ref, route_ref, cnt_out_ref, ltri_ref, cnt_ref)


def _sg_out_kernel(u_ref, vn_ref, ws_ref, bs_ref, x_ref, wo_ref, gf_ref, wr_ref, br_ref,
                   x1_ref, route_ref, cnt_out_ref, ltri_ref, cnt_ref, a_ref):
    tm, d = u_ref.shape
    r = lax.broadcasted_iota(jnp.int32, (LANES, LANES), 0)
    c = lax.broadcasted_iota(jnp.int32, (LANES, LANES), 1)
    keep = c <= r
    for g in range(d // LANES):
        sl = slice(g * LANES, (g + 1) * LANES)
        wc = jnp.where(keep, ws_ref[g], 0.0).astype(BF16)
        bias = bs_ref[:, g:g + 1]
        for ch in range(tm // LANES):
            rs = slice(ch * LANES, (ch + 1) * LANES)
            mixed = jnp.dot(wc, vn_ref[rs, sl], preferred_element_type=F32) + bias
            a_ref[rs, sl] = (u_ref[rs, sl].astype(F32) * mixed).astype(BF16)
    _mix_out_epilogue(a_ref[...], x_ref, wo_ref, gf_ref, wr_ref, br_ref,
                      x1_ref, route_ref, cnt_out_ref, ltri_ref, cnt_ref)


def _mix_out(x, w_out, g_ffn, w_router, b_router, *, attn_out=None, sg=None):
    t, d = x.shape
    tm = min(TM_ROW, t)
    row_spec = pl.BlockSpec((tm, d), lambda i: (i, 0))
    const = lambda shape: pl.BlockSpec(shape, lambda i: (0,) * len(shape))
    common_in = [row_spec, const((d, d)), const((1, d)), const(w_router.shape), const((1, LANES))]
    common_args = (x, w_out, g_ffn, w_router, b_router)
    out_specs = [row_spec, pl.BlockSpec((tm, LANES), lambda i: (i, 0)), const((1, LANES))]
    out_shape = [
        jax.ShapeDtypeStruct((t, d), F32),
        jax.ShapeDtypeStruct((t, LANES), F32),
        jax.ShapeDtypeStruct((1, LANES), F32),
    ]
    scratch = [pltpu.VMEM((tm, tm), BF16), pltpu.VMEM((1, LANES), F32)]
    if attn_out is not None:
        return pl.pallas_call(
            _attn_out_kernel,
            grid=(t // tm,),
            in_specs=[row_spec] + common_in,
            out_specs=out_specs, out_shape=out_shape, scratch_shapes=scratch,
            compiler_params=_cparams("arbitrary"),
            name="attn_out_router",
        )(attn_out, *common_args)
    z, w_s, b_s_t = sg
    return pl.pallas_call(
        _sg_out_kernel,
        grid=(t // tm,),
        in_specs=[
            pl.BlockSpec((tm, d), lambda i: (i, 0)),
            pl.BlockSpec((tm, d), lambda i: (i, 1)),
            const(w_s.shape),
            const(b_s_t.shape),
        ] + common_in,
        out_specs=out_specs, out_shape=out_shape,
        scratch_shapes=scratch + [pltpu.VMEM((tm, d), BF16)],
        compiler_params=_cparams("arbitrary"),
        name="sg_out_router",
    )(z, z, w_s, b_s_t, *common_args)


def _row_slice(ref, row):
    return ref.at[pl.ds(pl.multiple_of(row * ROW_TILES, ROW_TILES), ROW_TILES)]


def _for_each_token(n, fn, unroll=8):
    def group(o, carry):
        for u in range(unroll):
            fn(o * unroll + u)
        return carry
    lax.fori_loop(0, n // unroll, group, 0)


def _dispatch_kernel(pos_ref, pad_start_ref, pad_n_ref, n_used_ref, x1_ref, g_ref, xs_ref,
                     f32_ref, stage_ref, zero_ref, sems, pad_sem):
    i = pl.program_id(0)
    n_steps = pl.num_programs(0)
    td = x1_ref.shape[0]
    slot = i % 2

    def wait_slot(s):
        for _ in range(2):
            pltpu.make_async_copy(stage_ref.at[s], xs_ref.at[pl.ds(0, td * ROW_TILES)], sems.at[s]).wait()

    @pl.when(i == 0)
    def _():
        zero_ref[...] = jnp.zeros_like(zero_ref)
        tile_rows = zero_ref.shape[0]
        n_tiles = xs_ref.shape[0] // tile_rows

        def pad_copy(e, r):
            return pltpu.make_async_copy(zero_ref.at[pl.ds(0, ROW_TILES)],
                                         _row_slice(xs_ref, pad_start_ref[e] + r), pad_sem)

        def tile_copy(j):
            return pltpu.make_async_copy(
                zero_ref, xs_ref.at[pl.ds(pl.multiple_of(j * tile_rows, tile_rows), tile_rows)], pad_sem)

        def start_all(e, carry):
            lax.fori_loop(0, pad_n_ref[e], lambda r, c: (pad_copy(e, r).start(), c)[1], 0)
            return carry

        def wait_all(e, carry):
            lax.fori_loop(0, pad_n_ref[e], lambda r, c: (pad_copy(e, r).wait(), c)[1], 0)
            return carry

        lax.fori_loop(0, pad_n_ref.shape[0], start_all, 0)
        lax.fori_loop(n_used_ref[0], n_tiles, lambda j, c: (tile_copy(j).start(), c)[1], 0)
        lax.fori_loop(0, pad_n_ref.shape[0], wait_all, 0)
        lax.fori_loop(n_used_ref[0], n_tiles, lambda j, c: (tile_copy(j).wait(), c)[1], 0)

    @pl.when(i >= 2)
    def _():
        wait_slot(slot)

    _store_chunks(f32_ref, _rms(x1_ref[...], g_ref[...]))
    stage_ref[slot] = _token_major_from_halves(f32_ref).astype(BF16)

    def issue(t):
        src = stage_ref.at[slot, pl.ds(pl.multiple_of(t * ROW_TILES, ROW_TILES), ROW_TILES)]
        for k in range(2):
            pltpu.make_async_copy(src, _row_slice(xs_ref, pos_ref[0, 0, 2 * t + k]), sems.at[slot]).start()

    _for_each_token(td, issue)

    @pl.when(i == n_steps - 1)
    def _():
        wait_slot(slot)

    @pl.when((i == n_steps - 1) & (i >= 1))
    def _():
        wait_slot(1 - slot)


def _dispatch(x1, g_ffn, pos, pad_start, pad_n, n_used, n_sorted_rows):
    t, d = x1.shape
    td = min(TM_ROW, t)
    pos3 = pos.reshape(t // td, 1, 2 * td)
    smem = pl.BlockSpec(memory_space=pltpu.SMEM)
    return pl.pallas_call(
        _dispatch_kernel,
        grid=(t // td,),
        in_specs=[
            pl.BlockSpec((1, 1, 2 * td), lambda i: (i, 0, 0), memory_space=pltpu.SMEM),
            smem, smem, smem,
            pl.BlockSpec((td, d), lambda i: (i, 0)),
            pl.BlockSpec((1, d), lambda i: (0, 0)),
        ],
        out_specs=pl.BlockSpec(memory_space=pl.ANY),
        out_shape=jax.ShapeDtypeStruct((n_sorted_rows * ROW_TILES, LANES), BF16),
        scratch_shapes=[
            pltpu.VMEM((2, td * HALF_TILES, LANES), F32),
            pltpu.VMEM((2, td * ROW_TILES, LANES), BF16),
            pltpu.VMEM((TM_EXPERT * ROW_TILES, LANES), BF16),
            pltpu.SemaphoreType.DMA((2,)),
            pltpu.SemaphoreType.DMA(()),
        ],
        compiler_params=_cparams("arbitrary"),
        name="moe_dispatch",
    )(pos3, pad_start, pad_n, n_used, x1, g_ffn)


def _expert_kernel(tile_expert_ref, n_used_ref, xs_ref, wg_ref, wu_ref, wd_ref, ys_ref,
                   f32_ref, a_ref, wgu_bf_ref, wd_bf_ref):
    i = pl.program_id(0)
    tm, d = a_ref.shape
    n_row = d // LANES
    f = wd_ref.shape[0]

    @pl.when((i == 0) | (tile_expert_ref[i] != tile_expert_ref[jnp.maximum(i - 1, 0)]))
    def _():
        wgu_bf_ref[:, :f] = wg_ref[...].astype(BF16)
        wgu_bf_ref[:, f:] = wu_ref[...].astype(BF16)
        wd_bf_ref[...] = wd_ref[...].astype(BF16)

    @pl.when(i < n_used_ref[0])
    def _():
        _halves_from_token_major(f32_ref, xs_ref[...].astype(F32))
        for c in range(n_row):
            a_ref[:, c * LANES:(c + 1) * LANES] = _load_chunk(f32_ref, tm, c).astype(BF16)
        gu = jnp.dot(a_ref[...], wgu_bf_ref[...], preferred_element_type=F32)
        hidden = (jax.nn.silu(gu[:, :f]) * gu[:, f:]).astype(BF16)
        _store_chunks(f32_ref, jnp.dot(hidden, wd_bf_ref[...], preferred_element_type=F32))
        ys_ref[...] = _token_major_from_halves(f32_ref).astype(BF16)

    @pl.when(i >= n_used_ref[0])
    def _():
        ys_ref[...] = jnp.zeros_like(ys_ref)


def _experts(xs, layer, w_gate, w_up, w_down, tile_expert, n_used):
    _, e, d, f = w_gate.shape
    n_rows = xs.shape[0] // ROW_TILES
    tm = TM_EXPERT
    n_tiles = n_rows // tm
    return pl.pallas_call(
        _expert_kernel,
        grid_spec=pltpu.PrefetchScalarGridSpec(
            num_scalar_prefetch=2,
            grid=(n_tiles,),
            in_specs=[
                pl.BlockSpec((tm * ROW_TILES, LANES), lambda i, te, nu: (jnp.minimum(i, nu[0] - 1), 0)),
                pl.BlockSpec((None, None, d, f), lambda i, te, nu: (layer, te[i], 0, 0)),
                pl.BlockSpec((None, None, d, f), lambda i, te, nu: (layer, te[i], 0, 0)),
                pl.BlockSpec((None, None, f, d), lambda i, te, nu: (layer, te[i], 0, 0)),
            ],
            out_specs=pl.BlockSpec((tm * ROW_TILES, LANES), lambda i, te, nu: (i, 0)),
            scratch_shapes=[pltpu.VMEM((2, tm * HALF_TILES, LANES), F32), pltpu.VMEM((tm, d), BF16),
                            pltpu.VMEM((d, 2 * f), BF16), pltpu.VMEM((f, d), BF16)],
        ),
        out_shape=jax.ShapeDtypeStruct(xs.shape, BF16),
        compiler_params=_cparams("arbitrary"),
        name="moe_experts",
    )(tile_expert, n_used, xs, w_gate, w_up, w_down)


def _ple_kernel(pos_ref, pos_next_ref, x1_ref, route_ref, p_ref, gin_ref, wg_ref, wp_ref, gout_ref, ys_ref,
                o_ref, ybuf_ref, f32_ref, x2_ref, sems):
    i = pl.program_id(0)
    n_steps = pl.num_programs(0)
    tm, d = x1_ref.shape
    n_row = d // LANES
    slot = i % 2

    def gather(p_ref_, s):
        def issue(t):
            for k in range(2):
                dst = ybuf_ref.at[s, k, pl.ds(pl.multiple_of(t * ROW_TILES, ROW_TILES), ROW_TILES)]
                pltpu.make_async_copy(_row_slice(ys_ref, p_ref_[0, 0, 2 * t + k]), dst, sems.at[s]).start()
        _for_each_token(tm, issue)

    @pl.when(i == 0)
    def _():
        gather(pos_ref, 0)

    @pl.when(i + 1 < n_steps)
    def _():
        gather(pos_next_ref, 1 - slot)

    for k in range(2):
        pltpu.make_async_copy(ys_ref.at[pl.ds(0, tm * ROW_TILES)], ybuf_ref.at[slot, k], sems.at[slot]).wait()

    w = (route_ref[:, 4:5], route_ref[:, 5:6])
    for k in range(2):
        _halves_from_token_major(f32_ref.at[k], ybuf_ref[slot, k].astype(F32))
    for c in range(n_row):
        sl = slice(c * LANES, (c + 1) * LANES)
        y0 = _load_chunk(f32_ref.at[0], tm, c)
        y1 = _load_chunk(f32_ref.at[1], tm, c)
        x2_ref[:, sl] = x1_ref[:, sl] + w[0] * y0 + w[1] * y1
    x2 = x2_ref[...]
    gate = jax.nn.sigmoid(jnp.dot(_rms(x2, gin_ref[...]).astype(BF16), wg_ref[...],
                                  preferred_element_type=F32))
    e = jnp.dot(p_ref[...].astype(BF16), wp_ref[...], preferred_element_type=F32) * gate
    o_ref[...] = x2 + _rms(e, gout_ref[...])


def _ple(x1, ys, pos, route, p, g_in, w_gate, w_proj, g_out):
    t, d = x1.shape
    tm = min(TM_ROW, t)
    n_steps = t // tm
    pd = p.shape[1]
    pos3 = pos.reshape(n_steps, 1, 2 * tm)
    row_spec = pl.BlockSpec((tm, d), lambda i: (i, 0))
    const = lambda shape: pl.BlockSpec(shape, lambda i: (0,) * len(shape))
    return pl.pallas_call(
        _ple_kernel,
        grid=(n_steps,),
        in_specs=[
            pl.BlockSpec((1, 1, 2 * tm), lambda i: (i, 0, 0), memory_space=pltpu.SMEM),
            pl.BlockSpec((1, 1, 2 * tm), lambda i: (jnp.minimum(i + 1, n_steps - 1), 0, 0),
                         memory_space=pltpu.SMEM),
            row_spec,
            pl.BlockSpec((tm, LANES), lambda i: (i, 0)),
            pl.BlockSpec((tm, pd), lambda i: (i, 0)),
            const((1, d)), const((d, d)), const((pd, d)), const((1, d)),
            pl.BlockSpec(memory_space=pl.ANY),
        ],
        out_specs=row_spec,
        out_shape=jax.ShapeDtypeStruct((t, d), F32),
        scratch_shapes=[
            pltpu.VMEM((2, 2, tm * ROW_TILES, LANES), BF16),
            pltpu.VMEM((2, 2, tm * HALF_TILES, LANES), F32),
            pltpu.VMEM((tm, d), F32),
            pltpu.SemaphoreType.DMA((2,)),
        ],
        compiler_params=_cparams("arbitrary"),
        name="moe_combine_ple",
    )(pos3, pos3, x1, route, p, g_in, w_gate, w_proj, g_out, ys)


def _router_weights(w_group, b_group, w_expert, b_expert):
    d = w_group.shape[0]
    w = jnp.zeros((d, LANES), F32)
    w = w.at[:, :N_EXPERTS].set(w_expert).at[:, ROUTER_GROUP_LANE0:ROUTER_GROUP_LANE0 + N_GROUPS].set(w_group)
    b = jnp.zeros((1, LANES), F32)
    b = b.at[0, :N_EXPERTS].set(b_expert).at[0, ROUTER_GROUP_LANE0:ROUTER_GROUP_LANE0 + N_GROUPS].set(b_group)
    w_hi = w.astype(BF16)
    w_lo = (w - w_hi.astype(F32)).astype(BF16)
    return jnp.concatenate([w_hi, w_hi, w_lo], axis=0), b


def _sorted_layout(route, counts, t):
    tm = TM_EXPERT
    n_tiles = (2 * t + N_EXPERTS * (tm - 1)) // tm + 1
    eid = route[:, 0:2].astype(jnp.int32)
    rank = route[:, 2:4].astype(jnp.int32)
    cnt = counts[0, :N_EXPERTS].astype(jnp.int32)
    padded = ((cnt + tm - 1) // tm) * tm
    ends = jnp.cumsum(padded)
    offs = ends - padded
    pos = offs[eid] + rank
    tile_start = jnp.arange(n_tiles, dtype=jnp.int32) * tm
    n_used = (ends[-1] // tm).astype(jnp.int32)
    probe = jnp.minimum(tile_start, ends[-1] - tm)
    te = jnp.sum((probe[:, None] >= ends[None, :]).astype(jnp.int32), axis=1)
    return dict(pos=pos, tile_expert=te, n_used=n_used.reshape(1), pad_start=offs + cnt,
                pad_n=padded - cnt, n_sorted=n_tiles * tm)


def _moe_and_ple(x, mix_out_kwargs, w_out, g_ffn, router, expert_w, p_i, ple):
    t = x.shape[0]
    x1, route, counts = _mix_out(x, w_out, g_ffn, *router, **mix_out_kwargs)
    lay = _sorted_layout(route, counts, t)
    xs = _dispatch(x1, g_ffn, lay["pos"], lay["pad_start"], lay["pad_n"], lay["n_used"], lay["n_sorted"])
    ys = _experts(xs, *expert_w, lay["tile_expert"], lay["n_used"])
    return _ple(x1, ys, lay["pos"], route, p_i, *ple)


def kernel(x, p, norm_mix, norm_ffn, sb_w_in, sb_q_norm, sb_k_norm, sb_w_out, sg_w_in, sg_v_norm, sg_w_s, sg_b_s, sg_w_out, moe_w_group, moe_b_group, moe_w_expert, moe_b_expert, moe_w_gate, moe_w_up, moe_w_down, ple_norm_in, ple_w_gate, ple_w_proj, ple_norm_out):
    b, s, d = x.shape
    depth = norm_mix.shape[0]
    heads = d // LANES
    assert d == ROW_TILES * LANES
    t = b * s
    xt = x.reshape(t, d)
    row = lambda v: v.reshape(1, -1)
    for i in range(depth):
        j = i // 2
        router = _router_weights(moe_w_group[i], moe_b_group[i], moe_w_expert[i], moe_b_expert[i])
        expert_w = (i, moe_w_gate, moe_w_up, moe_w_down)
        ple = (row(ple_norm_in[i]), ple_w_gate[i].astype(BF16), ple_w_proj[i].astype(BF16), row(ple_norm_out[i]))
        if i % 2 == 0:
            q_gain = sb_q_norm[j] * (LANES ** -0.5 * LOG2E)
            colgain = jnp.concatenate([jnp.tile(q_gain, heads), jnp.tile(sb_k_norm[j], heads),
                                       jnp.ones((d,), F32)]).reshape(1, 3 * d)
            qkv = _qkv_proj(xt, row(norm_mix[i]), sb_w_in[j].astype(BF16), colgain)
            o = _stick_breaking(qkv.reshape(b, s, 3 * d), heads).reshape(t, d)
            mix = dict(attn_out=o)
            w_out = sb_w_out[j]
        else:
            z = _sg_in_proj(xt, row(norm_mix[i]), sg_w_in[j].astype(BF16), row(sg_v_norm[j]))
            mix = dict(sg=(z, sg_w_s[j], sg_b_s[j].T))
            w_out = sg_w_out[j]
        xt = _moe_and_ple(xt, mix, w_out.astype(BF16), row(norm_ffn[i]), router, expert_w,
                          p[i].reshape(t, -1), ple)
    return xt.reshape(b, s, d)
```

```python
import functools

import jax
import jax.numpy as jnp
from jax import lax
from jax.experimental import pallas as pl
from jax.experimental.pallas import tpu as pltpu

F32 = jnp.float32
BF16 = jnp.bfloat16

LANES = 128
ROW_TILES = 16
HALF_TILES = ROW_TILES // 2
VMEM_LIMIT_BYTES = 56 * 1024 * 1024
EPS = 1e-6
LOG2E = 1.4426950408889634
INV_LN2 = LOG2E
SIGN_BIT = -2 ** 31

N_GROUPS = 4
EXPERTS_PER_GROUP = 8
N_EXPERTS = N_GROUPS * EXPERTS_PER_GROUP
ROUTER_GROUP_LANE0 = N_EXPERTS

TM_PROJ = 512
TM_ROW = 256
TQ = 256
HEADS_PER_STEP = 4
TM_EXPERT = 256


def _cparams(*sem):
    return pltpu.CompilerParams(dimension_semantics=sem, vmem_limit_bytes=VMEM_LIMIT_BYTES)


def _rms(x, g):
    ms = jnp.mean(x * x, axis=-1, keepdims=True)
    return x * lax.rsqrt(ms + EPS) * g


def _qkv_kernel(x_ref, g_ref, w_ref, cg_ref, o_ref, xn_ref, *, n_norm_tiles):
    j = pl.program_id(1)

    @pl.when(j == 0)
    def _():
        xn_ref[...] = _rms(x_ref[...], g_ref[...]).astype(BF16)

    is_norm = j < n_norm_tiles
    acc = jnp.dot(xn_ref[...], w_ref[...], preferred_element_type=F32)
    for h in range(acc.shape[1] // LANES):
        sl = slice(h * LANES, (h + 1) * LANES)
        a = acc[:, sl]
        ms = jnp.mean(a * a, axis=-1, keepdims=True)
        scale = jnp.where(is_norm, lax.rsqrt(ms + EPS), 1.0)
        o_ref[:, sl] = (a * scale * cg_ref[:, sl]).astype(BF16)


def _qkv_proj(x, g, w, colgain, *, tn=1024):
    t, d = x.shape
    n = w.shape[1]
    tm = min(TM_PROJ, t)
    tn = min(tn, d)
    return pl.pallas_call(
        functools.partial(_qkv_kernel, n_norm_tiles=2 * d // tn),
        grid=(t // tm, n // tn),
        in_specs=[
            pl.BlockSpec((tm, d), lambda i, j: (i, 0)),
            pl.BlockSpec((1, d), lambda i, j: (0, 0)),
            pl.BlockSpec((d, tn), lambda i, j: (0, j)),
            pl.BlockSpec((1, tn), lambda i, j: (0, j)),
        ],
        out_specs=pl.BlockSpec((tm, tn), lambda i, j: (i, j)),
        out_shape=jax.ShapeDtypeStruct((t, n), BF16),
        scratch_shapes=[pltpu.VMEM((tm, d), BF16)],
        compiler_params=_cparams("parallel", "arbitrary"),
        name="qkv_proj",
    )(x, g, w, colgain)


def _sg_in_kernel(x_ref, g_ref, w_ref, vg_ref, o_ref, xn_ref):
    j = pl.program_id(1)

    @pl.when(j == 0)
    def _():
        xn_ref[...] = _rms(x_ref[...], g_ref[...]).astype(BF16)

    z = jax.nn.gelu(jnp.dot(xn_ref[...], w_ref[...], preferred_element_type=F32))
    is_v = j == 1
    ms = jnp.mean(z * z, axis=-1, keepdims=True)
    scale = jnp.where(is_v, lax.rsqrt(ms + EPS), 1.0)
    gain = jnp.where(is_v, vg_ref[...], 1.0)
    o_ref[...] = (z * scale * gain).astype(BF16)


def _sg_in_proj(x, g, w, v_gain):
    t, d = x.shape
    tm = min(TM_PROJ, t)
    return pl.pallas_call(
        _sg_in_kernel,
        grid=(t // tm, 2),
        in_specs=[
            pl.BlockSpec((tm, d), lambda i, j: (i, 0)),
            pl.BlockSpec((1, d), lambda i, j: (0, 0)),
            pl.BlockSpec((d, d), lambda i, j: (0, j)),
            pl.BlockSpec((1, d), lambda i, j: (0, 0)),
        ],
        out_specs=pl.BlockSpec((tm, d), lambda i, j: (i, j)),
        out_shape=jax.ShapeDtypeStruct((t, 2 * d), BF16),
        scratch_shapes=[pltpu.VMEM((tm, d), BF16)],
        compiler_params=_cparams("parallel", "arbitrary"),
        name="sg_in_proj",
    )(x, g, w, v_gain)


def _attn_kernel(q_ref, k_ref, v_ref, uu_ref, o_ref, acc_ref, carry_ref, za_ref, zb_ref):
    qi = pl.program_id(2)
    tq = q_ref.shape[0]
    tk = uu_ref.shape[1]
    n_heads = q_ref.shape[1] // LANES
    heads = range(n_heads)
    head = lambda ref_or_val, h: ref_or_val[:, h * LANES:(h + 1) * LANES]
    qs = [head(q_ref, h) for h in heads]

    def key_rows(kj):
        return pl.ds(pl.multiple_of(kj * tk, tk), tk)

    def store_logits(kj, z_out):
        k_all = k_ref[key_rows(kj), :]
        for h in heads:
            z_out[h] = lax.dot_general(qs[h], head(k_all, h), (((1,), (1,)), ((), ())),
                                       preferred_element_type=F32)

    def neg_abs(z):
        return lax.bitcast_convert_type(lax.bitcast_convert_type(z, jnp.int32) | SIGN_BIT, F32)

    def tile(kj, z_in, z_out, causal):
        store_logits(jnp.maximum(kj - 1, 0), z_out)
        v_all = v_ref[key_rows(kj), :]
        cs = []
        for h in heads:
            z = z_in[h]
            sp = jnp.maximum(z, 0.0) + jnp.log(1.0 + jnp.exp2(neg_abs(z))) * INV_LN2
            if causal is not None:
                sp = jnp.where(causal, sp, 0.0)
            cs.append(jnp.dot(sp.astype(BF16), uu_ref[...], preferred_element_type=F32))
        for h in heads:
            w = jnp.exp2(z_in[h] - cs[h])
            if causal is not None:
                w = jnp.where(causal, w, 0.0)
            pv = jnp.dot(w.astype(BF16), head(v_all, h), preferred_element_type=F32)
            carry = carry_ref[h]
            acc_ref[h] += jnp.exp2(-carry) * pv
            carry_ref[h] = carry + cs[h][:, 0:1]

    acc_ref[...] = jnp.zeros_like(acc_ref)
    carry_ref[...] = jnp.zeros_like(carry_ref)
    rows = lax.broadcasted_iota(jnp.int32, (tq, tk), 0)
    cols = lax.broadcasted_iota(jnp.int32, (tq, tk), 1)
    store_logits(qi, za_ref)
    tile(qi, za_ref, zb_ref, cols < rows)

    def pair(p, c):
        kj = qi - 1 - 2 * p
        tile(kj, zb_ref, za_ref, None)
        tile(kj - 1, za_ref, zb_ref, None)
        return c

    lax.fori_loop(0, qi // 2, pair, 0)

    @pl.when(qi % 2 == 1)
    def _():
        tile(0, zb_ref, za_ref, None)

    for h in heads:
        o_ref[:, h * LANES:(h + 1) * LANES] = acc_ref[h].astype(o_ref.dtype)


def _stick_breaking(qkv, heads):
    b, s, d3 = qkv.shape
    d = d3 // 3
    tq = min(TQ, s)
    tk = tq
    hw = HEADS_PER_STEP * LANES
    groups = heads // HEADS_PER_STEP
    j = lax.broadcasted_iota(jnp.int32, (tk, tk), 0)
    c = lax.broadcasted_iota(jnp.int32, (tk, tk), 1)
    uu = (j >= c).astype(BF16)
    return pl.pallas_call(
        _attn_kernel,
        grid=(b, groups, s // tq),
        in_specs=[
            pl.BlockSpec((None, tq, hw), lambda bi, h, i: (bi, i, h)),
            pl.BlockSpec((None, s, hw), lambda bi, h, i: (bi, 0, groups + h)),
            pl.BlockSpec((None, s, hw), lambda bi, h, i: (bi, 0, 2 * groups + h)),
            pl.BlockSpec((tk, tk), lambda bi, h, i: (0, 0)),
        ],
        out_specs=pl.BlockSpec((None, tq, hw), lambda bi, h, i: (bi, i, h)),
        out_shape=jax.ShapeDtypeStruct((b, s, d), BF16),
        scratch_shapes=[pltpu.VMEM((HEADS_PER_STEP, tq, LANES), F32),
                        pltpu.VMEM((HEADS_PER_STEP, tq, LANES), F32),
                        pltpu.VMEM((HEADS_PER_STEP, tq, tk), F32),
                        pltpu.VMEM((HEADS_PER_STEP, tq, tk), F32)],
        compiler_params=_cparams("parallel", "parallel", "arbitrary"),
        name="stick_breaking_attention",
    )(qkv, qkv, qkv, uu)


def _split_hi_lo(x):
    hi = x.astype(BF16)
    return hi, (x - hi.astype(F32)).astype(BF16)


def _route_tile(h2, wr_ref, br_ref, ltri_ref, cnt_ref):
    tm = h2.shape[0]
    hi, lo = _split_hi_lo(h2)
    lhs = jnp.concatenate([hi, lo, hi], axis=1)
    logits = jnp.dot(lhs, wr_ref[...], preferred_element_type=F32) + br_ref[...]

    lane = lax.broadcasted_iota(jnp.int32, (tm, LANES), 1)
    neg = jnp.float32(-jnp.inf)
    big = jnp.int32(LANES)

    is_group = (lane >= ROUTER_GROUP_LANE0) & (lane < ROUTER_GROUP_LANE0 + N_GROUPS)
    gl = jnp.where(is_group, logits, neg)
    gmax = jnp.max(gl, axis=-1, keepdims=True)
    g_idx = jnp.min(jnp.where(gl == gmax, lane - ROUTER_GROUP_LANE0, big), axis=-1, keepdims=True)
    g_w = 1.0 / jnp.sum(jnp.where(is_group, jnp.exp(gl - gmax), 0.0), axis=-1, keepdims=True)

    in_group = (lane < N_EXPERTS) & ((lane // EXPERTS_PER_GROUP) == g_idx)
    el = jnp.where(in_group, logits, neg)
    v1 = jnp.max(el, axis=-1, keepdims=True)
    i1 = jnp.min(jnp.where(el == v1, lane, big), axis=-1, keepdims=True)
    el2 = jnp.where(lane == i1, neg, el)
    v2 = jnp.max(el2, axis=-1, keepdims=True)
    i2 = jnp.min(jnp.where(el2 == v2, lane, big), axis=-1, keepdims=True)
    e21 = jnp.exp(v2 - v1)
    den = 1.0 + e21
    w1 = g_w * (1.0 / den)
    w2 = g_w * (e21 / den)

    oh1 = lane == i1
    oh2 = lane == i2
    onehot = (oh1 | oh2).astype(BF16)
    ahead = jnp.dot(ltri_ref[...], onehot, preferred_element_type=F32) + cnt_ref[...]
    r1 = jnp.sum(jnp.where(oh1, ahead, 0.0), axis=-1, keepdims=True)
    r2 = jnp.sum(jnp.where(oh2, ahead, 0.0), axis=-1, keepdims=True)
    cnt_ref[...] += jnp.sum(onehot.astype(F32), axis=0, keepdims=True)

    rec = jnp.where(lane == 0, i1.astype(F32), 0.0)
    rec = jnp.where(lane == 1, i2.astype(F32), rec)
    rec = jnp.where(lane == 2, r1, rec)
    rec = jnp.where(lane == 3, r2, rec)
    rec = jnp.where(lane == 4, w1, rec)
    rec = jnp.where(lane == 5, w2, rec)
    return rec


def _store_chunks(ref, val):
    tm = val.shape[0]
    for c in range(ROW_TILES):
        ref.at[c // HALF_TILES][pl.ds(c % HALF_TILES, tm, stride=HALF_TILES), :] = val[:, c * LANES:(c + 1) * LANES]


def _load_chunk(ref, tm, c):
    return ref.at[c // HALF_TILES][pl.ds(c % HALF_TILES, tm, stride=HALF_TILES), :]


def _halves_from_token_major(ref, val):
    tm = val.shape[0] // ROW_TILES
    v4 = val.reshape(tm, 2, HALF_TILES, LANES)
    for j in range(2):
        ref[j] = v4[:, j].reshape(tm * HALF_TILES, LANES)


def _token_major_from_halves(ref):
    tm = ref.shape[1] // HALF_TILES
    parts = [ref[j].reshape(tm, 1, HALF_TILES, LANES) for j in range(2)]
    return jnp.concatenate(parts, axis=1).reshape(tm * ROW_TILES, LANES)


def _mix_out_epilogue(a, x_ref, wo_ref, gf_ref, wr_ref, br_ref, x1_ref, route_ref, cnt_out_ref,
                      ltri_ref, cnt_ref):
    tm = a.shape[0]

    @pl.when(pl.program_id(0) == 0)
    def _():
        r = lax.broadcasted_iota(jnp.int32, (tm, tm), 0)
        c = lax.broadcasted_iota(jnp.int32, (tm, tm), 1)
        ltri_ref[...] = (c < r).astype(BF16)
        cnt_ref[...] = jnp.zeros_like(cnt_ref)

    x1 = x_ref[...] + jnp.dot(a, wo_ref[...], preferred_element_type=F32)
    x1_ref[...] = x1
    route_ref[...] = _route_tile(_rms(x1, gf_ref[...]), wr_ref, br_ref, ltri_ref, cnt_ref)
    cnt_out_ref[...] = cnt_ref[...]


def _attn_out_kernel(a_ref, x_ref, wo_ref, gf_ref, wr_ref, br_ref,
                     x1_ref, route_ref, cnt_out_ref, ltri_ref, cnt_ref):
    _mix_out_epilogue(a_ref[...], x_ref, wo_ref, gf_ref, wr_ref, br_ref,
                      x1_ref, route_ref, cnt_out_ref, ltri_ref, cnt_ref)


def _sg_out_kernel(u_ref, vn_ref, ws_ref, bs_ref, x_ref, wo_ref, gf_ref, wr_ref, br_ref,
                   x1_ref, route_ref, cnt_out_ref, ltri_ref, cnt_ref, a_ref):
    tm, d = u_ref.shape
    r = lax.broadcasted_iota(jnp.int32, (LANES, LANES), 0)
    c = lax.broadcasted_iota(jnp.int32, (LANES, LANES), 1)
    keep = c <= r
    for g in range(d // LANES):
        sl = slice(g * LANES, (g + 1) * LANES)
        wc = jnp.where(keep, ws_ref[g], 0.0).astype(BF16)
        bias = bs_ref[:, g:g + 1]
        for ch in range(tm // LANES):
            rs = slice(ch * LANES, (ch + 1) * LANES)
            mixed = jnp.dot(wc, vn_ref[rs, sl], preferred_element_type=F32) + bias
            a_ref[rs, sl] = (u_ref[rs, sl].astype(F32) * mixed).astype(BF16)
    _mix_out_epilogue(a_ref[...], x_ref, wo_ref, gf_ref, wr_ref, br_ref,
                      x1_ref, route_ref, cnt_out_ref, ltri_ref, cnt_ref)


def _mix_out(x, w_out, g_ffn, w_router, b_router, *, attn_out=None, sg=None):
    t, d = x.shape
    tm = min(TM_ROW, t)
    row_spec = pl.BlockSpec((tm, d), lambda i: (i, 0))
    const = lambda shape: pl.BlockSpec(shape, lambda i: (0,) * len(shape))
    common_in = [row_spec, const((d, d)), const((1, d)), const(w_router.shape), const((1, LANES))]
    common_args = (x, w_out, g_ffn, w_router, b_router)
    out_specs = [row_spec, pl.BlockSpec((tm, LANES), lambda i: (i, 0)), const((1, LANES))]
    out_shape = [
        jax.ShapeDtypeStruct((t, d), F32),
        jax.ShapeDtypeStruct((t, LANES), F32),
        jax.ShapeDtypeStruct((1, LANES), F32),
    ]
    scratch = [pltpu.VMEM((tm, tm), BF16), pltpu.VMEM((1, LANES), F32)]
    if attn_out is not None:
        return pl.pallas_call(
            _attn_out_kernel,
            grid=(t // tm,),
            in_specs=[row_spec] + common_in,
            out_specs=out_specs, out_shape=out_shape, scratch_shapes=scratch,
            compiler_params=_cparams("arbitrary"),
            name="attn_out_router",
        )(attn_out, *common_args)
    z, w_s, b_s_t = sg
    return pl.pallas_call(
        _sg_out_kernel,
        grid=(t // tm,),
        in_specs=[
            pl.BlockSpec((tm, d), lambda i: (i, 0)),
            pl.BlockSpec((tm, d), lambda i: (i, 1)),
            const(w_s.shape),
            const(b_s_t.shape),
        ] + common_in,
        out_specs=out_specs, out_shape=out_shape,
        scratch_shapes=scratch + [pltpu.VMEM((tm, d), BF16)],
        compiler_params=_cparams("arbitrary"),
        name="sg_out_router",
    )(z, z, w_s, b_s_t, *common_args)


def _row_slice(ref, row):
    return ref.at[pl.ds(pl.multiple_of(row * ROW_TILES, ROW_TILES), ROW_TILES)]


def _for_each_token(n, fn, unroll=8):
    def group(o, carry):
        for u in range(unroll):
            fn(o * unroll + u)
        return carry
    lax.fori_loop(0, n // unroll, group, 0)


def _dispatch_kernel(pos_ref, pad_start_ref, pad_n_ref, n_used_ref, x1_ref, g_ref, xs_ref,
                     f32_ref, stage_ref, zero_ref, sems, pad_sem):
    i = pl.program_id(0)
    n_steps = pl.num_programs(0)
    td = x1_ref.shape[0]
    slot = i % 2

    def wait_slot(s):
        for _ in range(2):
            pltpu.make_async_copy(stage_ref.at[s], xs_ref.at[pl.ds(0, td * ROW_TILES)], sems.at[s]).wait()

    @pl.when(i == 0)
    def _():
        zero_ref[...] = jnp.zeros_like(zero_ref)
        tile_rows = zero_ref.shape[0]
        n_tiles = xs_ref.shape[0] // tile_rows

        def pad_copy(e, r):
            return pltpu.make_async_copy(zero_ref.at[pl.ds(0, ROW_TILES)],
                                         _row_slice(xs_ref, pad_start_ref[e] + r), pad_sem)

        def tile_copy(j):
            return pltpu.make_async_copy(
                zero_ref, xs_ref.at[pl.ds(pl.multiple_of(j * tile_rows, tile_rows), tile_rows)], pad_sem)

        def start_all(e, carry):
            lax.fori_loop(0, pad_n_ref[e], lambda r, c: (pad_copy(e, r).start(), c)[1], 0)
            return carry

        def wait_all(e, carry):
            lax.fori_loop(0, pad_n_ref[e], lambda r, c: (pad_copy(e, r).wait(), c)[1], 0)
            return carry

        lax.fori_loop(0, pad_n_ref.shape[0], start_all, 0)
        lax.fori_loop(n_used_ref[0], n_tiles, lambda j, c: (tile_copy(j).start(), c)[1], 0)
        lax.fori_loop(0, pad_n_ref.shape[0], wait_all, 0)
        lax.fori_loop(n_used_ref[0], n_tiles, lambda j, c: (tile_copy(j).wait(), c)[1], 0)

    @pl.when(i >= 2)
    def _():
        wait_slot(slot)

    _store_chunks(f32_ref, _rms(x1_ref[...], g_ref[...]))
    stage_ref[slot] = _token_major_from_halves(f32_ref).astype(BF16)

    def issue(t):
        src = stage_ref.at[slot, pl.ds(pl.multiple_of(t * ROW_TILES, ROW_TILES), ROW_TILES)]
        for k in range(2):
            pltpu.make_async_copy(src, _row_slice(xs_ref, pos_ref[0, 0, 2 * t + k]), sems.at[slot]).start()

    _for_each_token(td, issue)

    @pl.when(i == n_steps - 1)
    def _():
        wait_slot(slot)

    @pl.when((i == n_steps - 1) & (i >= 1))
    def _():
        wait_slot(1 - slot)


def _dispatch(x1, g_ffn, pos, pad_start, pad_n, n_used, n_sorted_rows):
    t, d = x1.shape
    td = min(TM_ROW, t)
    pos3 = pos.reshape(t // td, 1, 2 * td)
    smem = pl.BlockSpec(memory_space=pltpu.SMEM)
    return pl.pallas_call(
        _dispatch_kernel,
        grid=(t // td,),
        in_specs=[
            pl.BlockSpec((1, 1, 2 * td), lambda i: (i, 0, 0), memory_space=pltpu.SMEM),
            smem, smem, smem,
            pl.BlockSpec((td, d), lambda i: (i, 0)),
            pl.BlockSpec((1, d), lambda i: (0, 0)),
        ],
        out_specs=pl.BlockSpec(memory_space=pl.ANY),
        out_shape=jax.ShapeDtypeStruct((n_sorted_rows * ROW_TILES, LANES), BF16),
        scratch_shapes=[
            pltpu.VMEM((2, td * HALF_TILES, LANES), F32),
            pltpu.VMEM((2, td * ROW_TILES, LANES), BF16),
            pltpu.VMEM((TM_EXPERT * ROW_TILES, LANES), BF16),
            pltpu.SemaphoreType.DMA((2,)),
            pltpu.SemaphoreType.DMA(()),
        ],
        compiler_params=_cparams("arbitrary"),
        name="moe_dispatch",
    )(pos3, pad_start, pad_n, n_used, x1, g_ffn)


def _expert_kernel(tile_expert_ref, n_used_ref, xs_ref, wg_ref, wu_ref, wd_ref, ys_ref,
                   f32_ref, a_ref, wgu_bf_ref, wd_bf_ref):
    i = pl.program_id(0)
    tm, d = a_ref.shape
    n_row = d // LANES
    f = wd_ref.shape[0]

    @pl.when((i == 0) | (tile_expert_ref[i] != tile_expert_ref[jnp.maximum(i - 1, 0)]))
    def _():
        wgu_bf_ref[:, :f] = wg_ref[...].astype(BF16)
        wgu_bf_ref[:, f:] = wu_ref[...].astype(BF16)
        wd_bf_ref[...] = wd_ref[...].astype(BF16)

    @pl.when(i < n_used_ref[0])
    def _():
        _halves_from_token_major(f32_ref, xs_ref[...].astype(F32))
        for c in range(n_row):
            a_ref[:, c * LANES:(c + 1) * LANES] = _load_chunk(f32_ref, tm, c).astype(BF16)
        gu = jnp.dot(a_ref[...], wgu_bf_ref[...], preferred_element_type=F32)
        hidden = (jax.nn.silu(gu[:, :f]) * gu[:, f:]).astype(BF16)
        _store_chunks(f32_ref, jnp.dot(hidden, wd_bf_ref[...], preferred_element_type=F32))
        ys_ref[...] = _token_major_from_halves(f32_ref).astype(BF16)

    @pl.when(i >= n_used_ref[0])
    def _():
        ys_ref[...] = jnp.zeros_like(ys_ref)


def _experts(xs, layer, w_gate, w_up, w_down, tile_expert, n_used):
    _, e, d, f = w_gate.shape
    n_rows = xs.shape[0] // ROW_TILES
    tm = TM_EXPERT
    n_tiles = n_rows // tm
    return pl.pallas_call(
        _expert_kernel,
        grid_spec=pltpu.PrefetchScalarGridSpec(
            num_scalar_prefetch=2,
            grid=(n_tiles,),
            in_specs=[
                pl.BlockSpec((tm * ROW_TILES, LANES), lambda i, te, nu: (jnp.minimum(i, nu[0] - 1), 0)),
                pl.BlockSpec((None, None, d, f), lambda i, te, nu: (layer, te[i], 0, 0)),
                pl.BlockSpec((None, None, d, f), lambda i, te, nu: (layer, te[i], 0, 0)),
                pl.BlockSpec((None, None, f, d), lambda i, te, nu: (layer, te[i], 0, 0)),
            ],
            out_specs=pl.BlockSpec((tm * ROW_TILES, LANES), lambda i, te, nu: (i, 0)),
            scratch_shapes=[pltpu.VMEM((2, tm * HALF_TILES, LANES), F32), pltpu.VMEM((tm, d), BF16),
                            pltpu.VMEM((d, 2 * f), BF16), pltpu.VMEM((f, d), BF16)],
        ),
        out_shape=jax.ShapeDtypeStruct(xs.shape, BF16),
        compiler_params=_cparams("arbitrary"),
        name="moe_experts",
    )(tile_expert, n_used, xs, w_gate, w_up, w_down)


def _ple_kernel(pos_ref, pos_next_ref, x1_ref, route_ref, p_ref, gin_ref, wg_ref, wp_ref, gout_ref, ys_ref,
                o_ref, ybuf_ref, f32_ref, x2_ref, sems):
    i = pl.program_id(0)
    n_steps = pl.num_programs(0)
    tm, d = x1_ref.shape
    n_row = d // LANES
    slot = i % 2

    def gather(p_ref_, s):
        def issue(t):
            for k in range(2):
                dst = ybuf_ref.at[s, k, pl.ds(pl.multiple_of(t * ROW_TILES, ROW_TILES), ROW_TILES)]
                pltpu.make_async_copy(_row_slice(ys_ref, p_ref_[0, 0, 2 * t + k]), dst, sems.at[s]).start()
        _for_each_token(tm, issue)

    @pl.when(i == 0)
    def _():
        gather(pos_ref, 0)

    @pl.when(i + 1 < n_steps)
    def _():
        gather(pos_next_ref, 1 - slot)

    for k in range(2):
        pltpu.make_async_copy(ys_ref.at[pl.ds(0, tm * ROW_TILES)], ybuf_ref.at[slot, k], sems.at[slot]).wait()

    w = (route_ref[:, 4:5], route_ref[:, 5:6])
    for k in range(2):
        _halves_from_token_major(f32_ref.at[k], ybuf_ref[slot, k].astype(F32))
    for c in range(n_row):
        sl = slice(c * LANES, (c + 1) * LANES)
        y0 = _load_chunk(f32_ref.at[0], tm, c)
        y1 = _load_chunk(f32_ref.at[1], tm, c)
        x2_ref[:, sl] = x1_ref[:, sl] + w[0] * y0 + w[1] * y1
    x2 = x2_ref[...]
    gate = jax.nn.sigmoid(jnp.dot(_rms(x2, gin_ref[...]).astype(BF16), wg_ref[...],
                                  preferred_element_type=F32))
    e = jnp.dot(p_ref[...].astype(BF16), wp_ref[...], preferred_element_type=F32) * gate
    o_ref[...] = x2 + _rms(e, gout_ref[...])


def _ple(x1, ys, pos, route, p, g_in, w_gate, w_proj, g_out):
    t, d = x1.shape
    tm = min(TM_ROW, t)
    n_steps = t // tm
    pd = p.shape[1]
    pos3 = pos.reshape(n_steps, 1, 2 * tm)
    row_spec = pl.BlockSpec((tm, d), lambda i: (i, 0))
    const = lambda shape: pl.BlockSpec(shape, lambda i: (0,) * len(shape))
    return pl.pallas_call(
        _ple_kernel,
        grid=(n_steps,),
        in_specs=[
            pl.BlockSpec((1, 1, 2 * tm), lambda i: (i, 0, 0), memory_space=pltpu.SMEM),
            pl.BlockSpec((1, 1, 2 * tm), lambda i: (jnp.minimum(i + 1, n_steps - 1), 0, 0),
                         memory_space=pltpu.SMEM),
            row_spec,
            pl.BlockSpec((tm, LANES), lambda i: (i, 0)),
            pl.BlockSpec((tm, pd), lambda i: (i, 0)),
            const((1, d)), const((d, d)), const((pd, d)), const((1, d)),
            pl.BlockSpec(memory_space=pl.ANY),
        ],
        out_specs=row_spec,
        out_shape=jax.ShapeDtypeStruct((t, d), F32),
        scratch_shapes=[
            pltpu.VMEM((2, 2, tm * ROW_TILES, LANES), BF16),
            pltpu.VMEM((2, 2, tm * HALF_TILES, LANES), F32),
            pltpu.VMEM((tm, d), F32),
            pltpu.SemaphoreType.DMA((2,)),
        ],
        compiler_params=_cparams("arbitrary"),
        name="moe_combine_ple",
    )(pos3, pos3, x1, route, p, g_in, w_gate, w_proj, g_out, ys)


def _router_weights(w_group, b_group, w_expert, b_expert):
    d = w_group.shape[0]
    w = jnp.zeros((d, LANES), F32)
    w = w.at[:, :N_EXPERTS].set(w_expert).at[:, ROUTER_GROUP_LANE0:ROUTER_GROUP_LANE0 + N_GROUPS].set(w_group)
    b = jnp.zeros((1, LANES), F32)
    b = b.at[0, :N_EXPERTS].set(b_expert).at[0, ROUTER_GROUP_LANE0:ROUTER_GROUP_LANE0 + N_GROUPS].set(b_group)
    w_hi = w.astype(BF16)
    w_lo = (w - w_hi.astype(F32)).astype(BF16)
    return jnp.concatenate([w_hi, w_hi, w_lo], axis=0), b


def _sorted_layout(route, counts, t):
    tm = TM_EXPERT
    n_tiles = (2 * t + N_EXPERTS * (tm - 1)) // tm + 1
    eid = route[:, 0:2].astype(jnp.int32)
    rank = route[:, 2:4].astype(jnp.int32)
    cnt = counts[0, :N_EXPERTS].astype(jnp.int32)
    padded = ((cnt + tm - 1) // tm) * tm
    ends = jnp.cumsum(padded)
    offs = ends - padded
    pos = offs[eid] + rank
    tile_start = jnp.arange(n_tiles, dtype=jnp.int32) * tm
    n_used = (ends[-1] // tm).astype(jnp.int32)
    probe = jnp.minimum(tile_start, ends[-1] - tm)
    te = jnp.sum((probe[:, None] >= ends[None, :]).astype(jnp.int32), axis=1)
    return dict(pos=pos, tile_expert=te, n_used=n_used.reshape(1), pad_start=offs + cnt,
                pad_n=padded - cnt, n_sorted=n_tiles * tm)


def _moe_and_ple(x, mix_out_kwargs, w_out, g_ffn, router, expert_w, p_i, ple):
    t = x.shape[0]
    x1, route, counts = _mix_out(x, w_out, g_ffn, *router, **mix_out_kwargs)
    lay = _sorted_layout(route, counts, t)
    xs = _dispatch(x1, g_ffn, lay["pos"], lay["pad_start"], lay["pad_n"], lay["n_used"], lay["n_sorted"])
    ys = _experts(xs, *expert_w, lay["tile_expert"], lay["n_used"])
    return _ple(x1, ys, lay["pos"], route, p_i, *ple)


def kernel(x, p, norm_mix, norm_ffn, sb_w_in, sb_q_norm, sb_k_norm, sb_w_out, sg_w_in, sg_v_norm, sg_w_s, sg_b_s, sg_w_out, moe_w_group, moe_b_group, moe_w_expert, moe_b_expert, moe_w_gate, moe_w_up, moe_w_down, ple_norm_in, ple_w_gate, ple_w_proj, ple_norm_out):
    b, s, d = x.shape
    depth = norm_mix.shape[0]
    heads = d // LANES
    assert d == ROW_TILES * LANES
    t = b * s
    xt = x.reshape(t, d)
    row = lambda v: v.reshape(1, -1)
    for i in range(depth):
        j = i // 2
        router = _router_weights(moe_w_group[i], moe_b_group[i], moe_w_expert[i], moe_b_expert[i])
        expert_w = (i, moe_w_gate, moe_w_up, moe_w_down)
        ple = (row(ple_norm_in[i]), ple_w_gate[i].astype(BF16), ple_w_proj[i].astype(BF16), row(ple_norm_out[i]))
        if i % 2 == 0:
            q_gain = sb_q_norm[j] * (LANES ** -0.5 * LOG2E)
            colgain = jnp.concatenate([jnp.tile(q_gain, heads), jnp.tile(sb_k_norm[j], heads),
                                       jnp.ones((d,), F32)]).reshape(1, 3 * d)
            qkv = _qkv_proj(xt, row(norm_mix[i]), sb_w_in[j].astype(BF16), colgain)
            o = _stick_breaking(qkv.reshape(b, s, 3 * d), heads).reshape(t, d)
            mix = dict(attn_out=o)
            w_out = sb_w_out[j]
        else:
            z = _sg_in_proj(xt, row(norm_mix[i]), sg_w_in[j].astype(BF16), row(sg_v_norm[j]))
            mix = dict(sg=(z, sg_w_s[j], sg_b_s[j].T))
            w_out = sg_w_out[j]
        xt = _moe_and_ple(xt, mix, w_out.astype(BF16), row(norm_ffn[i]), router, expert_w,
                          p[i].reshape(t, -1), ple)
    return xt.reshape(b, s, d)
```

```python
import functools

import jax
import jax.numpy as jnp
from jax import lax
from jax.experimental import pallas as pl
from jax.experimental.pallas import tpu as pltpu

F32 = jnp.float32
BF16 = jnp.bfloat16

LANES = 128
ROW_TILES = 16
HALF_TILES = ROW_TILES // 2
VMEM_LIMIT_BYTES = 56 * 1024 * 1024
EPS = 1e-6
LOG2E = 1.4426950408889634
INV_LN2 = LOG2E
SIGN_BIT = -2 ** 31

N_GROUPS = 4
EXPERTS_PER_GROUP = 8
N_EXPERTS = N_GROUPS * EXPERTS_PER_GROUP
ROUTER_GROUP_LANE0 = N_EXPERTS

TM_PROJ = 512
TM_ROW = 256
TQ = 256
HEADS_PER_STEP = 4
TM_EXPERT = 256


def _cparams(*sem):
    return pltpu.CompilerParams(dimension_semantics=sem, vmem_limit_bytes=VMEM_LIMIT_BYTES)


def _rms(x, g):
    ms = jnp.mean(x * x, axis=-1, keepdims=True)
    return x * lax.rsqrt(ms + EPS) * g


def _qkv_kernel(x_ref, g_ref, w_ref, cg_ref, o_ref, xn_ref, *, n_norm_tiles):
    j = pl.program_id(1)

    @pl.when(j == 0)
    def _():
        xn_ref[...] = _rms(x_ref[...], g_ref[...]).astype(BF16)

    is_norm = j < n_norm_tiles
    acc = jnp.dot(xn_ref[...], w_ref[...], preferred_element_type=F32)
    for h in range(acc.shape[1] // LANES):
        sl = slice(h * LANES, (h + 1) * LANES)
        a = acc[:, sl]
        ms = jnp.mean(a * a, axis=-1, keepdims=True)
        scale = jnp.where(is_norm, lax.rsqrt(ms + EPS), 1.0)
        o_ref[:, sl] = (a * scale * cg_ref[:, sl]).astype(BF16)


def _qkv_proj(x, g, w, colgain, *, tn=1024):
    t, d = x.shape
    n = w.shape[1]
    tm = min(TM_PROJ, t)
    tn = min(tn, d)
    return pl.pallas_call(
        functools.partial(_qkv_kernel, n_norm_tiles=2 * d // tn),
        grid=(t // tm, n // tn),
        in_specs=[
            pl.BlockSpec((tm, d), lambda i, j: (i, 0)),
            pl.BlockSpec((1, d), lambda i, j: (0, 0)),
            pl.BlockSpec((d, tn), lambda i, j: (0, j)),
            pl.BlockSpec((1, tn), lambda i, j: (0, j)),
        ],
        out_specs=pl.BlockSpec((tm, tn), lambda i, j: (i, j)),
        out_shape=jax.ShapeDtypeStruct((t, n), BF16),
        scratch_shapes=[pltpu.VMEM((tm, d), BF16)],
        compiler_params=_cparams("parallel", "arbitrary"),
        name="qkv_proj",
    )(x, g, w, colgain)


def _sg_in_kernel(x_ref, g_ref, w_ref, vg_ref, o_ref, xn_ref):
    j = pl.program_id(1)

    @pl.when(j == 0)
    def _():
        xn_ref[...] = _rms(x_ref[...], g_ref[...]).astype(BF16)

    z = jax.nn.gelu(jnp.dot(xn_ref[...], w_ref[...], preferred_element_type=F32))
    is_v = j == 1
    ms = jnp.mean(z * z, axis=-1, keepdims=True)
    scale = jnp.where(is_v, lax.rsqrt(ms + EPS), 1.0)
    gain = jnp.where(is_v, vg_ref[...], 1.0)
    o_ref[...] = (z * scale * gain).astype(BF16)


def _sg_in_proj(x, g, w, v_gain):
    t, d = x.shape
    tm = min(TM_PROJ, t)
    return pl.pallas_call(
        _sg_in_kernel,
        grid=(t // tm, 2),
        in_specs=[
            pl.BlockSpec((tm, d), lambda i, j: (i, 0)),
            pl.BlockSpec((1, d), lambda i, j: (0, 0)),
            pl.BlockSpec((d, d), lambda i, j: (0, j)),
            pl.BlockSpec((1, d), lambda i, j: (0, 0)),
        ],
        out_specs=pl.BlockSpec((tm, d), lambda i, j: (i, j)),
        out_shape=jax.ShapeDtypeStruct((t, 2 * d), BF16),
        scratch_shapes=[pltpu.VMEM((tm, d), BF16)],
        compiler_params=_cparams("parallel", "arbitrary"),
        name="sg_in_proj",
    )(x, g, w, v_gain)


def _attn_kernel(q_ref, k_ref, v_ref, uu_ref, o_ref, acc_ref, carry_ref, za_ref, zb_ref):
    qi = pl.program_id(2)
    tq = q_ref.shape[0]
    tk = uu_ref.shape[1]
    n_heads = q_ref.shape[1] // LANES
    heads = range(n_heads)
    head = lambda ref_or_val, h: ref_or_val[:, h * LANES:(h + 1) * LANES]
    qs = [head(q_ref, h) for h in heads]

    def key_rows(kj):
        return pl.ds(pl.multiple_of(kj * tk, tk), tk)

    def store_logits(kj, z_out):
        k_all = k_ref[key_rows(kj), :]
        for h in heads:
            z_out[h] = lax.dot_general(qs[h], head(k_all, h), (((1,), (1,)), ((), ())),
                                       preferred_element_type=F32)

    def neg_abs(z):
        return lax.bitcast_convert_type(lax.bitcast_convert_type(z, jnp.int32) | SIGN_BIT, F32)

    def tile(kj, z_in, z_out, causal):
        store_logits(jnp.maximum(kj - 1, 0), z_out)
        v_all = v_ref[key_rows(kj), :]
        cs = []
        for h in heads:
            z = z_in[h]
            sp = jnp.maximum(z, 0.0) + jnp.log(1.0 + jnp.exp2(neg_abs(z))) * INV_LN2
            if causal is not None:
                sp = jnp.where(causal, sp, 0.0)
            cs.append(jnp.dot(sp.astype(BF16), uu_ref[...], preferred_element_type=F32))
        for h in heads:
            w = jnp.exp2(z_in[h] - cs[h])
            if causal is not None:
                w = jnp.where(causal, w, 0.0)
            pv = jnp.dot(w.astype(BF16), head(v_all, h), preferred_element_type=F32)
            carry = carry_ref[h]
            acc_ref[h] += jnp.exp2(-carry) * pv
            carry_ref[h] = carry + cs[h][:, 0:1]

    acc_ref[...] = jnp.zeros_like(acc_ref)
    carry_ref[...] = jnp.zeros_like(carry_ref)
    rows = lax.broadcasted_iota(jnp.int32, (tq, tk), 0)
    cols = lax.broadcasted_iota(jnp.int32, (tq, tk), 1)
    store_logits(qi, za_ref)
    tile(qi, za_ref, zb_ref, cols < rows)

    def pair(kj):
        tile(kj, zb_ref, za_ref, None)
        tile(kj - 1, za_ref, zb_ref, None)

    def quad(p, c):
        kj = qi - 1 - 4 * p
        pair(kj)
        pair(kj - 2)
        return c

    lax.fori_loop(0, qi // 4, quad, 0)
    left = qi % 4

    @pl.when(left >= 2)
    def _():
        pair(left - 1)

    @pl.when(left % 2 == 1)
    def _():
        tile(0, zb_ref, za_ref, None)

    for h in heads:
        o_ref[:, h * LANES:(h + 1) * LANES] = acc_ref[h].astype(o_ref.dtype)


def _stick_breaking(qkv, heads):
    b, s, d3 = qkv.shape
    d = d3 // 3
    tq = min(TQ, s)
    tk = tq
    hw = HEADS_PER_STEP * LANES
    groups = heads // HEADS_PER_STEP
    j = lax.broadcasted_iota(jnp.int32, (tk, tk), 0)
    c = lax.broadcasted_iota(jnp.int32, (tk, tk), 1)
    uu = (j >= c).astype(BF16)
    return pl.pallas_call(
        _attn_kernel,
        grid=(b, groups, s // tq),
        in_specs=[
            pl.BlockSpec((None, tq, hw), lambda bi, h, i: (bi, i, h)),
            pl.BlockSpec((None, s, hw), lambda bi, h, i: (bi, 0, groups + h)),
            pl.BlockSpec((None, s, hw), lambda bi, h, i: (bi, 0, 2 * groups + h)),
            pl.BlockSpec((tk, tk), lambda bi, h, i: (0, 0)),
        ],
        out_specs=pl.BlockSpec((None, tq, hw), lambda bi, h, i: (bi, i, h)),
        out_shape=jax.ShapeDtypeStruct((b, s, d), BF16),
        scratch_shapes=[pltpu.VMEM((HEADS_PER_STEP, tq, LANES), F32),
                        pltpu.VMEM((HEADS_PER_STEP, tq, LANES), F32),
                        pltpu.VMEM((HEADS_PER_STEP, tq, tk), F32),
                        pltpu.VMEM((HEADS_PER_STEP, tq, tk), F32)],
        compiler_params=_cparams("parallel", "parallel", "arbitrary"),
        name="stick_breaking_attention",
    )(qkv, qkv, qkv, uu)


def _split_hi_lo(x):
    hi = x.astype(BF16)
    return hi, (x - hi.astype(F32)).astype(BF16)


def _route_tile(h2, wr_ref, br_ref, ltri_ref, cnt_ref):
    tm = h2.shape[0]
    hi, lo = _split_hi_lo(h2)
    lhs = jnp.concatenate([hi, lo, hi], axis=1)
    logits = jnp.dot(lhs, wr_ref[...], preferred_element_type=F32) + br_ref[...]

    lane = lax.broadcasted_iota(jnp.int32, (tm, LANES), 1)
    neg = jnp.float32(-jnp.inf)
    big = jnp.int32(LANES)

    is_group = (lane >= ROUTER_GROUP_LANE0) & (lane < ROUTER_GROUP_LANE0 + N_GROUPS)
    gl = jnp.where(is_group, logits, neg)
    gmax = jnp.max(gl, axis=-1, keepdims=True)
    g_idx = jnp.min(jnp.where(gl == gmax, lane - ROUTER_GROUP_LANE0, big), axis=-1, keepdims=True)
    g_w = 1.0 / jnp.sum(jnp.where(is_group, jnp.exp(gl - gmax), 0.0), axis=-1, keepdims=True)

    in_group = (lane < N_EXPERTS) & ((lane // EXPERTS_PER_GROUP) == g_idx)
    el = jnp.where(in_group, logits, neg)
    v1 = jnp.max(el, axis=-1, keepdims=True)
    i1 = jnp.min(jnp.where(el == v1, lane, big), axis=-1, keepdims=True)
    el2 = jnp.where(lane == i1, neg, el)
    v2 = jnp.max(el2, axis=-1, keepdims=True)
    i2 = jnp.min(jnp.where(el2 == v2, lane, big), axis=-1, keepdims=True)
    e21 = jnp.exp(v2 - v1)
    den = 1.0 + e21
    w1 = g_w * (1.0 / den)
    w2 = g_w * (e21 / den)

    oh1 = lane == i1
    oh2 = lane == i2
    onehot = (oh1 | oh2).astype(BF16)
    ahead = jnp.dot(ltri_ref[...], onehot, preferred_element_type=F32) + cnt_ref[...]
    r1 = jnp.sum(jnp.where(oh1, ahead, 0.0), axis=-1, keepdims=True)
    r2 = jnp.sum(jnp.where(oh2, ahead, 0.0), axis=-1, keepdims=True)
    cnt_ref[...] += jnp.sum(onehot.astype(F32), axis=0, keepdims=True)

    rec = jnp.where(lane == 0, i1.astype(F32), 0.0)
    rec = jnp.where(lane == 1, i2.astype(F32), rec)
    rec = jnp.where(lane == 2, r1, rec)
    rec = jnp.where(lane == 3, r2, rec)
    rec = jnp.where(lane == 4, w1, rec)
    rec = jnp.where(lane == 5, w2, rec)
    return rec


def _store_chunks(ref, val):
    tm = val.shape[0]
    for c in range(ROW_TILES):
        ref.at[c // HALF_TILES][pl.ds(c % HALF_TILES, tm, stride=HALF_TILES), :] = val[:, c * LANES:(c + 1) * LANES]


def _load_chunk(ref, tm, c):
    return ref.at[c // HALF_TILES][pl.ds(c % HALF_TILES, tm, stride=HALF_TILES), :]


def _halves_from_token_major(ref, val):
    tm = val.shape[0] // ROW_TILES
    v4 = val.reshape(tm, 2, HALF_TILES, LANES)
    for j in range(2):
        ref[j] = v4[:, j].reshape(tm * HALF_TILES, LANES)


def _token_major_from_halves(ref):
    tm = ref.shape[1] // HALF_TILES
    parts = [ref[j].reshape(tm, 1, HALF_TILES, LANES) for j in range(2)]
    return jnp.concatenate(parts, axis=1).reshape(tm * ROW_TILES, LANES)


def _mix_out_epilogue(a, x_ref, wo_ref, gf_ref, wr_ref, br_ref, x1_ref, route_ref, cnt_out_ref,
                      ltri_ref, cnt_ref):
    tm = a.shape[0]

    @pl.when(pl.program_id(0) == 0)
    def _():
        r = lax.broadcasted_iota(jnp.int32, (tm, tm), 0)
        c = lax.broadcasted_iota(jnp.int32, (tm, tm), 1)
        ltri_ref[...] = (c < r).astype(BF16)
        cnt_ref[...] = jnp.zeros_like(cnt_ref)

    x1 = x_ref[...] + jnp.dot(a, wo_ref[...], preferred_element_type=F32)
    x1_ref[...] = x1
    route_ref[...] = _route_tile(_rms(x1, gf_ref[...]), wr_ref, br_ref, ltri_ref, cnt_ref)
    cnt_out_ref[...] = cnt_ref[...]


def _attn_out_kernel(a_ref, x_ref, wo_ref, gf_ref, wr_ref, br_ref,
                     x1_ref, route_ref, cnt_out_ref, ltri_ref, cnt_ref):
    _mix_out_epilogue(a_ref[...], x_ref, wo_ref, gf_ref, wr_ref, br_ref,
                      x1_ref, route_ref, cnt_out_ref, ltri_ref, cnt_ref)


def _sg_out_kernel(u_ref, vn_ref, ws_ref, bs_ref, x_ref, wo_ref, gf_ref, wr_ref, br_ref,
                   x1_ref, route_ref, cnt_out_ref, ltri_ref, cnt_ref, a_ref):
    tm, d = u_ref.shape
    r = lax.broadcasted_iota(jnp.int32, (LANES, LANES), 0)
    c = lax.broadcasted_iota(jnp.int32, (LANES, LANES), 1)
    keep = c <= r
    for g in range(d // LANES):
        sl = slice(g * LANES, (g + 1) * LANES)
        wc = jnp.where(keep, ws_ref[g], 0.0).astype(BF16)
        bias = bs_ref[:, g:g + 1]
        for ch in range(tm // LANES):
            rs = slice(ch * LANES, (ch + 1) * LANES)
            mixed = jnp.dot(wc, vn_ref[rs, sl], preferred_element_type=F32) + bias
            a_ref[rs, sl] = (u_ref[rs, sl].astype(F32) * mixed).astype(BF16)
    _mix_out_epilogue(a_ref[...], x_ref, wo_ref, gf_ref, wr_ref, br_ref,
                      x1_ref, route_ref, cnt_out_ref, ltri_ref, cnt_ref)


def _mix_out(x, w_out, g_ffn, w_router, b_router, *, attn_out=None, sg=None):
    t, d = x.shape
    tm = min(TM_ROW, t)
    row_spec = pl.BlockSpec((tm, d), lambda i: (i, 0))
    const = lambda shape: pl.BlockSpec(shape, lambda i: (0,) * len(shape))
    common_in = [row_spec, const((d, d)), const((1, d)), const(w_router.shape), const((1, LANES))]
    common_args = (x, w_out, g_ffn, w_router, b_router)
    out_specs = [row_spec, pl.BlockSpec((tm, LANES), lambda i: (i, 0)), const((1, LANES))]
    out_shape = [
        jax.ShapeDtypeStruct((t, d), F32),
        jax.ShapeDtypeStruct((t, LANES), F32),
        jax.ShapeDtypeStruct((1, LANES), F32),
    ]
    scratch = [pltpu.VMEM((tm, tm), BF16), pltpu.VMEM((1, LANES), F32)]
    if attn_out is not None:
        return pl.pallas_call(
            _attn_out_kernel,
            grid=(t // tm,),
            in_specs=[row_spec] + common_in,
            out_specs=out_specs, out_shape=out_shape, scratch_shapes=scratch,
            compiler_params=_cparams("arbitrary"),
            name="attn_out_router",
        )(attn_out, *common_args)
    z, w_s, b_s_t = sg
    return pl.pallas_call(
        _sg_out_kernel,
        grid=(t // tm,),
        in_specs=[
            pl.BlockSpec((tm, d), lambda i: (i, 0)),
            pl.BlockSpec((tm, d), lambda i: (i, 1)),
            const(w_s.shape),
            const(b_s_t.shape),
        ] + common_in,
        out_specs=out_specs, out_shape=out_shape,
        scratch_shapes=scratch + [pltpu.VMEM((tm, d), BF16)],
        compiler_params=_cparams("arbitrary"),
        name="sg_out_router",
    )(z, z, w_s, b_s_t, *common_args)


def _row_slice(ref, row):
    return ref.at[pl.ds(pl.multiple_of(row * ROW_TILES, ROW_TILES), ROW_TILES)]


def _for_each_token(n, fn, unroll=8):
    def group(o, carry):
        for u in range(unroll):
            fn(o * unroll + u)
        return carry
    lax.fori_loop(0, n // unroll, group, 0)


def _dispatch_kernel(pos_ref, pad_start_ref, pad_n_ref, n_used_ref, x1_ref, g_ref, xs_ref,
                     f32_ref, stage_ref, zero_ref, sems, pad_sem):
    i = pl.program_id(0)
    n_steps = pl.num_programs(0)
    td = x1_ref.shape[0]
    slot = i % 2

    def wait_slot(s):
        for _ in range(2):
            pltpu.make_async_copy(stage_ref.at[s], xs_ref.at[pl.ds(0, td * ROW_TILES)], sems.at[s]).wait()

    @pl.when(i == 0)
    def _():
        zero_ref[...] = jnp.zeros_like(zero_ref)
        tile_rows = zero_ref.shape[0]
        n_tiles = xs_ref.shape[0] // tile_rows

        def pad_copy(e, r):
            return pltpu.make_async_copy(zero_ref.at[pl.ds(0, ROW_TILES)],
                                         _row_slice(xs_ref, pad_start_ref[e] + r), pad_sem)

        def tile_copy(j):
            return pltpu.make_async_copy(
                zero_ref, xs_ref.at[pl.ds(pl.multiple_of(j * tile_rows, tile_rows), tile_rows)], pad_sem)

        def start_all(e, carry):
            lax.fori_loop(0, pad_n_ref[e], lambda r, c: (pad_copy(e, r).start(), c)[1], 0)
            return carry

        def wait_all(e, carry):
            lax.fori_loop(0, pad_n_ref[e], lambda r, c: (pad_copy(e, r).wait(), c)[1], 0)
            return carry

        lax.fori_loop(0, pad_n_ref.shape[0], start_all, 0)
        lax.fori_loop(n_used_ref[0], n_tiles, lambda j, c: (tile_copy(j).start(), c)[1], 0)
        lax.fori_loop(0, pad_n_ref.shape[0], wait_all, 0)
        lax.fori_loop(n_used_ref[0], n_tiles, lambda j, c: (tile_copy(j).wait(), c)[1], 0)

    @pl.when(i >= 2)
    def _():
        wait_slot(slot)

    _store_chunks(f32_ref, _rms(x1_ref[...], g_ref[...]))
    stage_ref[slot] = _token_major_from_halves(f32_ref).astype(BF16)

    def issue(t):
        src = stage_ref.at[slot, pl.ds(pl.multiple_of(t * ROW_TILES, ROW_TILES), ROW_TILES)]
        for k in range(2):
            pltpu.make_async_copy(src, _row_slice(xs_ref, pos_ref[0, 0, 2 * t + k]), sems.at[slot]).start()

    _for_each_token(td, issue)

    @pl.when(i == n_steps - 1)
    def _():
        wait_slot(slot)

    @pl.when((i == n_steps - 1) & (i >= 1))
    def _():
        wait_slot(1 - slot)


def _dispatch(x1, g_ffn, pos, pad_start, pad_n, n_used, n_sorted_rows):
    t, d = x1.shape
    td = min(TM_ROW, t)
    pos3 = pos.reshape(t // td, 1, 2 * td)
    smem = pl.BlockSpec(memory_space=pltpu.SMEM)
    return pl.pallas_call(
        _dispatch_kernel,
        grid=(t // td,),
        in_specs=[
            pl.BlockSpec((1, 1, 2 * td), lambda i: (i, 0, 0), memory_space=pltpu.SMEM),
            smem, smem, smem,
            pl.BlockSpec((td, d), lambda i: (i, 0)),
            pl.BlockSpec((1, d), lambda i: (0, 0)),
        ],
        out_specs=pl.BlockSpec(memory_space=pl.ANY),
        out_shape=jax.ShapeDtypeStruct((n_sorted_rows * ROW_TILES, LANES), BF16),
        scratch_shapes=[
            pltpu.VMEM((2, td * HALF_TILES, LANES), F32),
            pltpu.VMEM((2, td * ROW_TILES, LANES), BF16),
            pltpu.VMEM((TM_EXPERT * ROW_TILES, LANES), BF16),
            pltpu.SemaphoreType.DMA((2,)),
            pltpu.SemaphoreType.DMA(()),
        ],
        compiler_params=_cparams("arbitrary"),
        name="moe_dispatch",
    )(pos3, pad_start, pad_n, n_used, x1, g_ffn)


def _expert_kernel(tile_expert_ref, n_used_ref, xs_ref, wg_ref, wu_ref, wd_ref, ys_ref,
                   f32_ref, a_ref, wgu_bf_ref, wd_bf_ref):
    i = pl.program_id(0)
    tm, d = a_ref.shape
    n_row = d // LANES
    f = wd_ref.shape[0]

    @pl.when((i == 0) | (tile_expert_ref[i] != tile_expert_ref[jnp.maximum(i - 1, 0)]))
    def _():
        wgu_bf_ref[:, :f] = wg_ref[...].astype(BF16)
        wgu_bf_ref[:, f:] = wu_ref[...].astype(BF16)
        wd_bf_ref[...] = wd_ref[...].astype(BF16)

    @pl.when(i < n_used_ref[0])
    def _():
        _halves_from_token_major(f32_ref, xs_ref[...].astype(F32))
        for c in range(n_row):
            a_ref[:, c * LANES:(c + 1) * LANES] = _load_chunk(f32_ref, tm, c).astype(BF16)
        gu = jnp.dot(a_ref[...], wgu_bf_ref[...], preferred_element_type=F32)
        hidden = (jax.nn.silu(gu[:, :f]) * gu[:, f:]).astype(BF16)
        _store_chunks(f32_ref, jnp.dot(hidden, wd_bf_ref[...], preferred_element_type=F32))
        ys_ref[...] = _token_major_from_halves(f32_ref).astype(BF16)

    @pl.when(i >= n_used_ref[0])
    def _():
        ys_ref[...] = jnp.zeros_like(ys_ref)


def _experts(xs, layer, w_gate, w_up, w_down, tile_expert, n_used):
    _, e, d, f = w_gate.shape
    n_rows = xs.shape[0] // ROW_TILES
    tm = TM_EXPERT
    n_tiles = n_rows // tm
    return pl.pallas_call(
        _expert_kernel,
        grid_spec=pltpu.PrefetchScalarGridSpec(
            num_scalar_prefetch=2,
            grid=(n_tiles,),
            in_specs=[
                pl.BlockSpec((tm * ROW_TILES, LANES), lambda i, te, nu: (jnp.minimum(i, nu[0] - 1), 0)),
                pl.BlockSpec((None, None, d, f), lambda i, te, nu: (layer, te[i], 0, 0)),
                pl.BlockSpec((None, None, d, f), lambda i, te, nu: (layer, te[i], 0, 0)),
                pl.BlockSpec((None, None, f, d), lambda i, te, nu: (layer, te[i], 0, 0)),
            ],
            out_specs=pl.BlockSpec((tm * ROW_TILES, LANES), lambda i, te, nu: (i, 0)),
            scratch_shapes=[pltpu.VMEM((2, tm * HALF_TILES, LANES), F32), pltpu.VMEM((tm, d), BF16),
                            pltpu.VMEM((d, 2 * f), BF16), pltpu.VMEM((f, d), BF16)],
        ),
        out_shape=jax.ShapeDtypeStruct(xs.shape, BF16),
        compiler_params=_cparams("arbitrary"),
        name="moe_experts",
    )(tile_expert, n_used, xs, w_gate, w_up, w_down)


def _ple_kernel(pos_ref, pos_next_ref, x1_ref, route_ref, p_ref, gin_ref, wg_ref, wp_ref, gout_ref, ys_ref,
                o_ref, ybuf_ref, f32_ref, x2_ref, sems):
    i = pl.program_id(0)
    n_steps = pl.num_programs(0)
    tm, d = x1_ref.shape
    n_row = d // LANES
    slot = i % 2

    def issue(p_ref_, s, t):
        for k in range(2):
            dst = ybuf_ref.at[s, k, pl.ds(pl.multiple_of(t * ROW_TILES, ROW_TILES), ROW_TILES)]
            pltpu.make_async_copy(_row_slice(ys_ref, p_ref_[0, 0, 2 * t + k]), dst, sems.at[s]).start()

    def wait_rows(s):
        for k in range(2):
            pltpu.make_async_copy(ys_ref.at[pl.ds(0, tm * ROW_TILES)], ybuf_ref.at[s, k], sems.at[s]).wait()

    @pl.when(i == 0)
    def _():
        _for_each_token(tm, functools.partial(issue, pos_ref, 0))

    wait_rows(slot)

    w = (route_ref[:, 4:5], route_ref[:, 5:6])
    for k in range(2):
        _halves_from_token_major(f32_ref.at[k], ybuf_ref[slot, k].astype(F32))
    tokens_per_chunk = tm // n_row
    for c in range(n_row):
        for t in range(c * tokens_per_chunk, (c + 1) * tokens_per_chunk):
            issue(pos_next_ref, 1 - slot, t)
        sl = slice(c * LANES, (c + 1) * LANES)
        y0 = _load_chunk(f32_ref.at[0], tm, c)
        y1 = _load_chunk(f32_ref.at[1], tm, c)
        x2_ref[:, sl] = x1_ref[:, sl] + w[0] * y0 + w[1] * y1
    x2 = x2_ref[...]
    gate = jax.nn.sigmoid(jnp.dot(_rms(x2, gin_ref[...]).astype(BF16), wg_ref[...],
                                  preferred_element_type=F32))
    e = jnp.dot(p_ref[...].astype(BF16), wp_ref[...], preferred_element_type=F32) * gate
    o_ref[...] = x2 + _rms(e, gout_ref[...])

    @pl.when(i == n_steps - 1)
    def _():
        wait_rows(1 - slot)


def _ple(x1, ys, pos, route, p, g_in, w_gate, w_proj, g_out):
    t, d = x1.shape
    tm = min(TM_ROW, t)
    n_steps = t // tm
    pd = p.shape[1]
    pos3 = pos.reshape(n_steps, 1, 2 * tm)
    row_spec = pl.BlockSpec((tm, d), lambda i: (i, 0))
    const = lambda shape: pl.BlockSpec(shape, lambda i: (0,) * len(shape))
    return pl.pallas_call(
        _ple_kernel,
        grid=(n_steps,),
        in_specs=[
            pl.BlockSpec((1, 1, 2 * tm), lambda i: (i, 0, 0), memory_space=pltpu.SMEM),
            pl.BlockSpec((1, 1, 2 * tm), lambda i: (jnp.minimum(i + 1, n_steps - 1), 0, 0),
                         memory_space=pltpu.SMEM),
            row_spec,
            pl.BlockSpec((tm, LANES), lambda i: (i, 0)),
            pl.BlockSpec((tm, pd), lambda i: (i, 0)),
            const((1, d)), const((d, d)), const((pd, d)), const((1, d)),
            pl.BlockSpec(memory_space=pl.ANY),
        ],
        out_specs=row_spec,
        out_shape=jax.ShapeDtypeStruct((t, d), F32),
        scratch_shapes=[
            pltpu.VMEM((2, 2, tm * ROW_TILES, LANES), BF16),
            pltpu.VMEM((2, 2, tm * HALF_TILES, LANES), F32),
            pltpu.VMEM((tm, d), F32),
            pltpu.SemaphoreType.DMA((2,)),
        ],
        compiler_params=_cparams("arbitrary"),
        name="moe_combine_ple",
    )(pos3, pos3, x1, route, p, g_in, w_gate, w_proj, g_out, ys)


def _router_weights(w_group, b_group, w_expert, b_expert):
    d = w_group.shape[0]
    pad = LANES - N_EXPERTS - N_GROUPS
    w = jnp.concatenate([w_expert, w_group, jnp.zeros((d, pad), F32)], axis=1)
    b = jnp.concatenate([b_expert, b_group, jnp.zeros((pad,), F32)]).reshape(1, LANES)
    w_hi = w.astype(BF16)
    w_lo = (w - w_hi.astype(F32)).astype(BF16)
    return jnp.concatenate([w_hi, w_hi, w_lo], axis=0), b


def _sorted_layout(route, counts, t):
    tm = TM_EXPERT
    n_tiles = (2 * t + N_EXPERTS * (tm - 1)) // tm + 1
    eid = route[:, 0:2].astype(jnp.int32)
    rank = route[:, 2:4].astype(jnp.int32)
    cnt = counts[0, :N_EXPERTS].astype(jnp.int32)
    padded = ((cnt + tm - 1) // tm) * tm
    ends = jnp.cumsum(padded)
    offs = ends - padded
    pos = offs[eid] + rank
    tile_start = jnp.arange(n_tiles, dtype=jnp.int32) * tm
    n_used = (ends[-1] // tm).astype(jnp.int32)
    probe = jnp.minimum(tile_start, ends[-1] - tm)
    te = jnp.sum((probe[:, None] >= ends[None, :]).astype(jnp.int32), axis=1)
    return dict(pos=pos, tile_expert=te, n_used=n_used.reshape(1), pad_start=offs + cnt,
                pad_n=padded - cnt, n_sorted=n_tiles * tm)


def _moe_and_ple(x, mix_out_kwargs, w_out, g_ffn, router, expert_w, p_i, ple):
    t = x.shape[0]
    x1, route, counts = _mix_out(x, w_out, g_ffn, *router, **mix_out_kwargs)
    lay = _sorted_layout(route, counts, t)
    xs = _dispatch(x1, g_ffn, lay["pos"], lay["pad_start"], lay["pad_n"], lay["n_used"], lay["n_sorted"])
    ys = _experts(xs, *expert_w, lay["tile_expert"], lay["n_used"])
    return _ple(x1, ys, lay["pos"], route, p_i, *ple)


def kernel(x, p, norm_mix, norm_ffn, sb_w_in, sb_q_norm, sb_k_norm, sb_w_out, sg_w_in, sg_v_norm, sg_w_s, sg_b_s, sg_w_out, moe_w_group, moe_b_group, moe_w_expert, moe_b_expert, moe_w_gate, moe_w_up, moe_w_down, ple_norm_in, ple_w_gate, ple_w_proj, ple_norm_out):
    b, s, d = x.shape
    depth = norm_mix.shape[0]
    heads = d // LANES
    assert d == ROW_TILES * LANES
    t = b * s
    xt = x.reshape(t, d)
    row = lambda v: v.reshape(1, -1)
    for i in range(depth):
        j = i // 2
        router = _router_weights(moe_w_group[i], moe_b_group[i], moe_w_expert[i], moe_b_expert[i])
        expert_w = (i, moe_w_gate, moe_w_up, moe_w_down)
        ple = (row(ple_norm_in[i]), ple_w_gate[i].astype(BF16), ple_w_proj[i].astype(BF16), row(ple_norm_out[i]))
        if i % 2 == 0:
            q_gain = sb_q_norm[j] * (LANES ** -0.5 * LOG2E)
            colgain = jnp.concatenate([jnp.tile(q_gain, heads), jnp.tile(sb_k_norm[j], heads),
                                       jnp.ones((d,), F32)]).reshape(1, 3 * d)
            qkv = _qkv_proj(xt, row(norm_mix[i]), sb_w_in[j].astype(BF16), colgain)
            o = _stick_breaking(qkv.reshape(b, s, 3 * d), heads).reshape(t, d)
            mix = dict(attn_out=o)
            w_out = sb_w_out[j]
        else:
            z = _sg_in_proj(xt, row(norm_mix[i]), sg_w_in[j].astype(BF16), row(sg_v_norm[j]))
            mix = dict(sg=(z, sg_w_s[j], sg_b_s[j].T))
            w_out = sg_w_out[j]
        xt = _moe_and_ple(xt, mix, w_out.astype(BF16), row(norm_ffn[i]), router, expert_w,
                          p[i].reshape(t, -1), ple)
    return xt.reshape(b, s, d)
```

```python
import functools

import jax
import jax.numpy as jnp
from jax import lax
from jax.experimental import pallas as pl
from jax.experimental.pallas import tpu as pltpu

F32 = jnp.float32
BF16 = jnp.bfloat16

LANES = 128
ROW_TILES = 16
HALF_TILES = ROW_TILES // 2
VMEM_LIMIT_BYTES = 56 * 1024 * 1024
EPS = 1e-6
LOG2E = 1.4426950408889634
INV_LN2 = LOG2E
SIGN_BIT = -2 ** 31

N_GROUPS = 4
EXPERTS_PER_GROUP = 8
N_EXPERTS = N_GROUPS * EXPERTS_PER_GROUP
ROUTER_GROUP_LANE0 = N_EXPERTS

TM_PROJ = 512
TM_ROW = 256
TQ = 256
HEADS_PER_STEP = 4
TM_EXPERT = 256


def _cparams(*sem):
    return pltpu.CompilerParams(dimension_semantics=sem, vmem_limit_bytes=VMEM_LIMIT_BYTES)


def _rms(x, g):
    ms = jnp.mean(x * x, axis=-1, keepdims=True)
    return x * lax.rsqrt(ms + EPS) * g


def _qkv_kernel(x_ref, g_ref, w_ref, cg_ref, o_ref, xn_ref, *, n_norm_tiles):
    j = pl.program_id(1)

    @pl.when(j == 0)
    def _():
        xn_ref[...] = _rms(x_ref[...], g_ref[...]).astype(BF16)

    is_norm = j < n_norm_tiles
    acc = jnp.dot(xn_ref[...], w_ref[...], preferred_element_type=F32)
    for h in range(acc.shape[1] // LANES):
        sl = slice(h * LANES, (h + 1) * LANES)
        a = acc[:, sl]
        ms = jnp.mean(a * a, axis=-1, keepdims=True)
        scale = jnp.where(is_norm, lax.rsqrt(ms + EPS), 1.0)
        o_ref[:, sl] = (a * scale * cg_ref[:, sl]).astype(BF16)


def _qkv_proj(x, g, w, colgain, *, tn=1024):
    t, d = x.shape
    n = w.shape[1]
    tm = min(TM_PROJ, t)
    tn = min(tn, d)
    return pl.pallas_call(
        functools.partial(_qkv_kernel, n_norm_tiles=2 * d // tn),
        grid=(t // tm, n // tn),
        in_specs=[
            pl.BlockSpec((tm, d), lambda i, j: (i, 0)),
            pl.BlockSpec((1, d), lambda i, j: (0, 0)),
            pl.BlockSpec((d, tn), lambda i, j: (0, j)),
            pl.BlockSpec((1, tn), lambda i, j: (0, j)),
        ],
        out_specs=pl.BlockSpec((tm, tn), lambda i, j: (i, j)),
        out_shape=jax.ShapeDtypeStruct((t, n), BF16),
        scratch_shapes=[pltpu.VMEM((tm, d), BF16)],
        compiler_params=_cparams("parallel", "arbitrary"),
        name="qkv_proj",
    )(x, g, w, colgain)


def _sg_in_kernel(x_ref, g_ref, w_ref, vg_ref, o_ref, xn_ref):
    j = pl.program_id(1)

    @pl.when(j == 0)
    def _():
        xn_ref[...] = _rms(x_ref[...], g_ref[...]).astype(BF16)

    z = jax.nn.gelu(jnp.dot(xn_ref[...], w_ref[...], preferred_element_type=F32))
    is_v = j == 1
    ms = jnp.mean(z * z, axis=-1, keepdims=True)
    scale = jnp.where(is_v, lax.rsqrt(ms + EPS), 1.0)
    gain = jnp.where(is_v, vg_ref[...], 1.0)
    o_ref[...] = (z * scale * gain).astype(BF16)


def _sg_in_proj(x, g, w, v_gain):
    t, d = x.shape
    tm = min(TM_PROJ, t)
    return pl.pallas_call(
        _sg_in_kernel,
        grid=(t // tm, 2),
        in_specs=[
            pl.BlockSpec((tm, d), lambda i, j: (i, 0)),
            pl.BlockSpec((1, d), lambda i, j: (0, 0)),
            pl.BlockSpec((d, d), lambda i, j: (0, j)),
            pl.BlockSpec((1, d), lambda i, j: (0, 0)),
        ],
        out_specs=pl.BlockSpec((tm, d), lambda i, j: (i, j)),
        out_shape=jax.ShapeDtypeStruct((t, 2 * d), BF16),
        scratch_shapes=[pltpu.VMEM((tm, d), BF16)],
        compiler_params=_cparams("parallel", "arbitrary"),
        name="sg_in_proj",
    )(x, g, w, v_gain)


def _attn_kernel(q_ref, k_ref, v_ref, uu_ref, o_ref, acc_ref, carry_ref, za_ref, zb_ref):
    qi = pl.program_id(2)
    tq = q_ref.shape[0]
    tk = uu_ref.shape[1]
    n_heads = q_ref.shape[1] // LANES
    heads = range(n_heads)
    head = lambda ref_or_val, h: ref_or_val[:, h * LANES:(h + 1) * LANES]
    qs = [head(q_ref, h) for h in heads]

    def key_rows(kj):
        return pl.ds(pl.multiple_of(kj * tk, tk), tk)

    def store_logits(kj, z_out):
        k_all = k_ref[key_rows(kj), :]
        for h in heads:
            z_out[h] = lax.dot_general(qs[h], head(k_all, h), (((1,), (1,)), ((), ())),
                                       preferred_element_type=F32)

    def neg_abs(z):
        return lax.bitcast_convert_type(lax.bitcast_convert_type(z, jnp.int32) | SIGN_BIT, F32)

    def tile(kj, z_in, z_out, causal):
        store_logits(jnp.maximum(kj - 1, 0), z_out)
        v_all = v_ref[key_rows(kj), :]
        cs = []
        for h in heads:
            z = z_in[h]
            sp = jnp.maximum(z, 0.0) + jnp.log(1.0 + jnp.exp2(neg_abs(z))) * INV_LN2
            if causal is not None:
                sp = jnp.where(causal, sp, 0.0)
            cs.append(jnp.dot(sp.astype(BF16), uu_ref[...], preferred_element_type=F32))
        for h in heads:
            w = jnp.exp2(z_in[h] - cs[h])
            if causal is not None:
                w = jnp.where(causal, w, 0.0)
            pv = jnp.dot(w.astype(BF16), head(v_all, h), preferred_element_type=F32)
            carry = carry_ref[h]
            acc_ref[h] += jnp.exp2(-carry) * pv
            carry_ref[h] = carry + cs[h][:, 0:1]

    acc_ref[...] = jnp.zeros_like(acc_ref)
    carry_ref[...] = jnp.zeros_like(carry_ref)
    rows = lax.broadcasted_iota(jnp.int32, (tq, tk), 0)
    cols = lax.broadcasted_iota(jnp.int32, (tq, tk), 1)
    store_logits(qi, za_ref)
    tile(qi, za_ref, zb_ref, cols < rows)

    def pair(kj):
        tile(kj, zb_ref, za_ref, None)
        tile(kj - 1, za_ref, zb_ref, None)

    def quad(p, c):
        kj = qi - 1 - 4 * p
        pair(kj)
        pair(kj - 2)
        return c

    lax.fori_loop(0, qi // 4, quad, 0)
    left = qi % 4

    @pl.when(left >= 2)
    def _():
        pair(left - 1)

    @pl.when(left % 2 == 1)
    def _():
        tile(0, zb_ref, za_ref, None)

    for h in heads:
        o_ref[:, h * LANES:(h + 1) * LANES] = acc_ref[h].astype(o_ref.dtype)


def _stick_breaking(qkv, heads):
    b, s, d3 = qkv.shape
    d = d3 // 3
    tq = min(TQ, s)
    tk = tq
    hw = HEADS_PER_STEP * LANES
    groups = heads // HEADS_PER_STEP
    j = lax.broadcasted_iota(jnp.int32, (tk, tk), 0)
    c = lax.broadcasted_iota(jnp.int32, (tk, tk), 1)
    uu = (j >= c).astype(BF16)
    return pl.pallas_call(
        _attn_kernel,
        grid=(b, groups, s // tq),
        in_specs=[
            pl.BlockSpec((None, tq, hw), lambda bi, h, i: (bi, i, h)),
            pl.BlockSpec((None, s, hw), lambda bi, h, i: (bi, 0, groups + h)),
            pl.BlockSpec((None, s, hw), lambda bi, h, i: (bi, 0, 2 * groups + h)),
            pl.BlockSpec((tk, tk), lambda bi, h, i: (0, 0)),
        ],
        out_specs=pl.BlockSpec((None, tq, hw), lambda bi, h, i: (bi, i, h)),
        out_shape=jax.ShapeDtypeStruct((b, s, d), BF16),
        scratch_shapes=[pltpu.VMEM((HEADS_PER_STEP, tq, LANES), F32),
                        pltpu.VMEM((HEADS_PER_STEP, tq, LANES), F32),
                        pltpu.VMEM((HEADS_PER_STEP, tq, tk), F32),
                        pltpu.VMEM((HEADS_PER_STEP, tq, tk), F32)],
        compiler_params=_cparams("parallel", "parallel", "arbitrary"),
        name="stick_breaking_attention",
    )(qkv, qkv, qkv, uu)


def _split_hi_lo(x):
    hi = x.astype(BF16)
    return hi, (x - hi.astype(F32)).astype(BF16)


def _route_tile(h2, wr_ref, br_ref, ltri_ref, cnt_ref):
    tm = h2.shape[0]
    hi, lo = _split_hi_lo(h2)
    lhs = jnp.concatenate([hi, lo, hi], axis=1)
    logits = jnp.dot(lhs, wr_ref[...], preferred_element_type=F32) + br_ref[...]

    lane = lax.broadcasted_iota(jnp.int32, (tm, LANES), 1)
    neg = jnp.float32(-jnp.inf)
    big = jnp.int32(LANES)

    is_group = (lane >= ROUTER_GROUP_LANE0) & (lane < ROUTER_GROUP_LANE0 + N_GROUPS)
    gl = jnp.where(is_group, logits, neg)
    gmax = jnp.max(gl, axis=-1, keepdims=True)
    g_idx = jnp.min(jnp.where(gl == gmax, lane - ROUTER_GROUP_LANE0, big), axis=-1, keepdims=True)
    g_w = 1.0 / jnp.sum(jnp.where(is_group, jnp.exp(gl - gmax), 0.0), axis=-1, keepdims=True)

    in_group = (lane < N_EXPERTS) & ((lane // EXPERTS_PER_GROUP) == g_idx)
    el = jnp.where(in_group, logits, neg)
    v1 = jnp.max(el, axis=-1, keepdims=True)
    i1 = jnp.min(jnp.where(el == v1, lane, big), axis=-1, keepdims=True)
    el2 = jnp.where(lane == i1, neg, el)
    v2 = jnp.max(el2, axis=-1, keepdims=True)
    i2 = jnp.min(jnp.where(el2 == v2, lane, big), axis=-1, keepdims=True)
    e21 = jnp.exp(v2 - v1)
    den = 1.0 + e21
    w1 = g_w * (1.0 / den)
    w2 = g_w * (e21 / den)

    oh1 = lane == i1
    oh2 = lane == i2
    onehot = (oh1 | oh2).astype(BF16)
    ahead = jnp.dot(ltri_ref[...], onehot, preferred_element_type=F32) + cnt_ref[...]
    r1 = jnp.sum(jnp.where(oh1, ahead, 0.0), axis=-1, keepdims=True)
    r2 = jnp.sum(jnp.where(oh2, ahead, 0.0), axis=-1, keepdims=True)
    cnt_ref[...] += jnp.sum(onehot.astype(F32), axis=0, keepdims=True)

    rec = jnp.where(lane == 0, i1.astype(F32), 0.0)
    rec = jnp.where(lane == 1, i2.astype(F32), rec)
    rec = jnp.where(lane == 2, r1, rec)
    rec = jnp.where(lane == 3, r2, rec)
    rec = jnp.where(lane == 4, w1, rec)
    rec = jnp.where(lane == 5, w2, rec)
    return rec


def _store_chunks(ref, val):
    tm = val.shape[0]
    for c in range(ROW_TILES):
        ref.at[c // HALF_TILES][pl.ds(c % HALF_TILES, tm, stride=HALF_TILES), :] = val[:, c * LANES:(c + 1) * LANES]


def _load_chunk(ref, tm, c):
    return ref.at[c // HALF_TILES][pl.ds(c % HALF_TILES, tm, stride=HALF_TILES), :]


def _halves_from_token_major(ref, val):
    tm = val.shape[0] // ROW_TILES
    v4 = val.reshape(tm, 2, HALF_TILES, LANES)
    for j in range(2):
        ref[j] = v4[:, j].reshape(tm * HALF_TILES, LANES)


def _token_major_from_halves(ref):
    tm = ref.shape[1] // HALF_TILES
    parts = [ref[j].reshape(tm, 1, HALF_TILES, LANES) for j in range(2)]
    return jnp.concatenate(parts, axis=1).reshape(tm * ROW_TILES, LANES)


def _mix_out_epilogue(a, x_ref, wo_ref, gf_ref, wr_ref, br_ref, x1_ref, route_ref, cnt_out_ref,
                      ltri_ref, cnt_ref):
    tm = a.shape[0]

    @pl.when(pl.program_id(0) == 0)
    def _():
        r = lax.broadcasted_iota(jnp.int32, (tm, tm), 0)
        c = lax.broadcasted_iota(jnp.int32, (tm, tm), 1)
        ltri_ref[...] = (c < r).astype(BF16)
        cnt_ref[...] = jnp.zeros_like(cnt_ref)

    x1 = x_ref[...] + jnp.dot(a, wo_ref[...], preferred_element_type=F32)
    x1_ref[...] = x1
    route_ref[...] = _route_tile(_rms(x1, gf_ref[...]), wr_ref, br_ref, ltri_ref, cnt_ref)
    cnt_out_ref[...] = cnt_ref[...]


def _attn_out_kernel(a_ref, x_ref, wo_ref, gf_ref, wr_ref, br_ref,
                     x1_ref, route_ref, cnt_out_ref, ltri_ref, cnt_ref):
    _mix_out_epilogue(a_ref[...], x_ref, wo_ref, gf_ref, wr_ref, br_ref,
                      x1_ref, route_ref, cnt_out_ref, ltri_ref, cnt_ref)


def _sg_out_kernel(u_ref, vn_ref, ws_ref, bs_ref, x_ref, wo_ref, gf_ref, wr_ref, br_ref,
                   x1_ref, route_ref, cnt_out_ref, ltri_ref, cnt_ref, a_ref):
    tm, d = u_ref.shape
    r = lax.broadcasted_iota(jnp.int32, (LANES, LANES), 0)
    c = lax.broadcasted_iota(jnp.int32, (LANES, LANES), 1)
    keep = c <= r
    for g in range(d // LANES):
        sl = slice(g * LANES, (g + 1) * LANES)
        wc = jnp.where(keep, ws_ref[g], 0.0).astype(BF16)
        bias = bs_ref[:, g:g + 1]
        for ch in range(tm // LANES):
            rs = slice(ch * LANES, (ch + 1) * LANES)
            mixed = jnp.dot(wc, vn_ref[rs, sl], preferred_element_type=F32) + bias
            a_ref[rs, sl] = (u_ref[rs, sl].astype(F32) * mixed).astype(BF16)
    _mix_out_epilogue(a_ref[...], x_ref, wo_ref, gf_ref, wr_ref, br_ref,
                      x1_ref, route_ref, cnt_out_ref, ltri_ref, cnt_ref)


def _mix_out(x, w_out, g_ffn, w_router, b_router, *, attn_out=None, sg=None):
    t, d = x.shape
    tm = min(TM_ROW, t)
    row_spec = pl.BlockSpec((tm, d), lambda i: (i, 0))
    const = lambda shape: pl.BlockSpec(shape, lambda i: (0,) * len(shape))
    common_in = [row_spec, const((d, d)), const((1, d)), const(w_router.shape), const((1, LANES))]
    common_args = (x, w_out, g_ffn, w_router, b_router)
    out_specs = [row_spec, pl.BlockSpec((tm, LANES), lambda i: (i, 0)), const((1, LANES))]
    out_shape = [
        jax.ShapeDtypeStruct((t, d), F32),
        jax.ShapeDtypeStruct((t, LANES), F32),
        jax.ShapeDtypeStruct((1, LANES), F32),
    ]
    scratch = [pltpu.VMEM((tm, tm), BF16), pltpu.VMEM((1, LANES), F32)]
    if attn_out is not None:
        return pl.pallas_call(
            _attn_out_kernel,
            grid=(t // tm,),
            in_specs=[row_spec] + common_in,
            out_specs=out_specs, out_shape=out_shape, scratch_shapes=scratch,
            compiler_params=_cparams("arbitrary"),
            name="attn_out_router",
        )(attn_out, *common_args)
    z, w_s, b_s_t = sg
    return pl.pallas_call(
        _sg_out_kernel,
        grid=(t // tm,),
        in_specs=[
            pl.BlockSpec((tm, d), lambda i: (i, 0)),
            pl.BlockSpec((tm, d), lambda i: (i, 1)),
            const(w_s.shape),
            const(b_s_t.shape),
        ] + common_in,
        out_specs=out_specs, out_shape=out_shape,
        scratch_shapes=scratch + [pltpu.VMEM((tm, d), BF16)],
        compiler_params=_cparams("arbitrary"),
        name="sg_out_router",
    )(z, z, w_s, b_s_t, *common_args)


def _row_slice(ref, row):
    return ref.at[pl.ds(pl.multiple_of(row * ROW_TILES, ROW_TILES), ROW_TILES)]


def _for_each_token(n, fn, unroll=8):
    def group(o, carry):
        for u in range(unroll):
            fn(o * unroll + u)
        return carry
    lax.fori_loop(0, n // unroll, group, 0)


def _dispatch_kernel(pos_ref, pad_start_ref, pad_n_ref, n_used_ref, x1_ref, g_ref, xs_ref,
                     f32_ref, stage_ref, zero_ref, sems, pad_sem):
    i = pl.program_id(0)
    n_steps = pl.num_programs(0)
    td = x1_ref.shape[0]
    slot = i % 2

    def wait_slot(s):
        for _ in range(2):
            pltpu.make_async_copy(stage_ref.at[s], xs_ref.at[pl.ds(0, td * ROW_TILES)], sems.at[s]).wait()

    @pl.when(i == 0)
    def _():
        zero_ref[...] = jnp.zeros_like(zero_ref)
        tile_rows = zero_ref.shape[0]
        n_tiles = xs_ref.shape[0] // tile_rows

        def pad_copy(e, r):
            return pltpu.make_async_copy(zero_ref.at[pl.ds(0, ROW_TILES)],
                                         _row_slice(xs_ref, pad_start_ref[e] + r), pad_sem)

        def tile_copy(j):
            return pltpu.make_async_copy(
                zero_ref, xs_ref.at[pl.ds(pl.multiple_of(j * tile_rows, tile_rows), tile_rows)], pad_sem)

        def start_all(e, carry):
            lax.fori_loop(0, pad_n_ref[e], lambda r, c: (pad_copy(e, r).start(), c)[1], 0)
            return carry

        def wait_all(e, carry):
            lax.fori_loop(0, pad_n_ref[e], lambda r, c: (pad_copy(e, r).wait(), c)[1], 0)
            return carry

        lax.fori_loop(0, pad_n_ref.shape[0], start_all, 0)
        lax.fori_loop(n_used_ref[0], n_tiles, lambda j, c: (tile_copy(j).start(), c)[1], 0)
        lax.fori_loop(0, pad_n_ref.shape[0], wait_all, 0)
        lax.fori_loop(n_used_ref[0], n_tiles, lambda j, c: (tile_copy(j).wait(), c)[1], 0)

    @pl.when(i >= 2)
    def _():
        wait_slot(slot)

    _store_chunks(f32_ref, _rms(x1_ref[...], g_ref[...]))
    stage_ref[slot] = _token_major_from_halves(f32_ref).astype(BF16)

    def issue(t):
        src = stage_ref.at[slot, pl.ds(pl.multiple_of(t * ROW_TILES, ROW_TILES), ROW_TILES)]
        for k in range(2):
            pltpu.make_async_copy(src, _row_slice(xs_ref, pos_ref[0, 0, k * td + t]), sems.at[slot]).start()

    _for_each_token(td, issue)

    @pl.when(i == n_steps - 1)
    def _():
        wait_slot(slot)

    @pl.when((i == n_steps - 1) & (i >= 1))
    def _():
        wait_slot(1 - slot)


def _dispatch(x1, g_ffn, pos, pad_start, pad_n, n_used, n_sorted_rows):
    t, d = x1.shape
    td = min(TM_ROW, t)
    pos3 = pos.reshape(t // td, 1, 2 * td)
    smem = pl.BlockSpec(memory_space=pltpu.SMEM)
    return pl.pallas_call(
        _dispatch_kernel,
        grid=(t // td,),
        in_specs=[
            pl.BlockSpec((1, 1, 2 * td), lambda i: (i, 0, 0), memory_space=pltpu.SMEM),
            smem, smem, smem,
            pl.BlockSpec((td, d), lambda i: (i, 0)),
            pl.BlockSpec((1, d), lambda i: (0, 0)),
        ],
        out_specs=pl.BlockSpec(memory_space=pl.ANY),
        out_shape=jax.ShapeDtypeStruct((n_sorted_rows * ROW_TILES, LANES), BF16),
        scratch_shapes=[
            pltpu.VMEM((2, td * HALF_TILES, LANES), F32),
            pltpu.VMEM((2, td * ROW_TILES, LANES), BF16),
            pltpu.VMEM((TM_EXPERT * ROW_TILES, LANES), BF16),
            pltpu.SemaphoreType.DMA((2,)),
            pltpu.SemaphoreType.DMA(()),
        ],
        compiler_params=_cparams("arbitrary"),
        name="moe_dispatch",
    )(pos3, pad_start, pad_n, n_used, x1, g_ffn)


def _expert_kernel(tile_expert_ref, n_used_ref, xs_ref, wg_ref, wu_ref, wd_ref, ys_ref,
                   f32_ref, a_ref, wgu_bf_ref, wd_bf_ref):
    i = pl.program_id(0)
    tm, d = a_ref.shape
    n_row = d // LANES
    f = wd_ref.shape[0]

    @pl.when((i == 0) | (tile_expert_ref[i] != tile_expert_ref[jnp.maximum(i - 1, 0)]))
    def _():
        wgu_bf_ref[:, :f] = wg_ref[...].astype(BF16)
        wgu_bf_ref[:, f:] = wu_ref[...].astype(BF16)
        wd_bf_ref[...] = wd_ref[...].astype(BF16)

    @pl.when(i < n_used_ref[0])
    def _():
        _halves_from_token_major(f32_ref, xs_ref[...].astype(F32))
        for c in range(n_row):
            a_ref[:, c * LANES:(c + 1) * LANES] = _load_chunk(f32_ref, tm, c).astype(BF16)
        gu = jnp.dot(a_ref[...], wgu_bf_ref[...], preferred_element_type=F32)
        hidden = (jax.nn.silu(gu[:, :f]) * gu[:, f:]).astype(BF16)
        _store_chunks(f32_ref, jnp.dot(hidden, wd_bf_ref[...], preferred_element_type=F32))
        ys_ref[...] = _token_major_from_halves(f32_ref).astype(BF16)

    @pl.when(i >= n_used_ref[0])
    def _():
        ys_ref[...] = jnp.zeros_like(ys_ref)


def _experts(xs, layer, w_gate, w_up, w_down, tile_expert, n_used):
    _, e, d, f = w_gate.shape
    n_rows = xs.shape[0] // ROW_TILES
    tm = TM_EXPERT
    n_tiles = n_rows // tm
    return pl.pallas_call(
        _expert_kernel,
        grid_spec=pltpu.PrefetchScalarGridSpec(
            num_scalar_prefetch=2,
            grid=(n_tiles,),
            in_specs=[
                pl.BlockSpec((tm * ROW_TILES, LANES), lambda i, te, nu: (jnp.minimum(i, nu[0] - 1), 0)),
                pl.BlockSpec((None, None, d, f), lambda i, te, nu: (layer, te[i], 0, 0)),
                pl.BlockSpec((None, None, d, f), lambda i, te, nu: (layer, te[i], 0, 0)),
                pl.BlockSpec((None, None, f, d), lambda i, te, nu: (layer, te[i], 0, 0)),
            ],
            out_specs=pl.BlockSpec((tm * ROW_TILES, LANES), lambda i, te, nu: (i, 0)),
            scratch_shapes=[pltpu.VMEM((2, tm * HALF_TILES, LANES), F32), pltpu.VMEM((tm, d), BF16),
                            pltpu.VMEM((d, 2 * f), BF16), pltpu.VMEM((f, d), BF16)],
        ),
        out_shape=jax.ShapeDtypeStruct(xs.shape, BF16),
        compiler_params=_cparams("arbitrary"),
        name="moe_experts",
    )(tile_expert, n_used, xs, w_gate, w_up, w_down)


def _ple_kernel(pos_ref, pos_next_ref, x1_ref, route_ref, p_ref, gin_ref, wg_ref, wp_ref, gout_ref, ys_ref,
                o_ref, ybuf_ref, f32_ref, x2_ref, sems):
    i = pl.program_id(0)
    n_steps = pl.num_programs(0)
    tm, d = x1_ref.shape
    n_row = d // LANES
    slot = i % 2

    def gather(p_ref_, s):
        def issue(t):
            for k in range(2):
                dst = ybuf_ref.at[s, k, pl.ds(pl.multiple_of(t * ROW_TILES, ROW_TILES), ROW_TILES)]
                pltpu.make_async_copy(_row_slice(ys_ref, p_ref_[0, 0, k * tm + t]), dst, sems.at[s]).start()
        _for_each_token(tm, issue)

    @pl.when(i == 0)
    def _():
        gather(pos_ref, 0)

    @pl.when(i + 1 < n_steps)
    def _():
        gather(pos_next_ref, 1 - slot)

    for k in range(2):
        pltpu.make_async_copy(ys_ref.at[pl.ds(0, tm * ROW_TILES)], ybuf_ref.at[slot, k], sems.at[slot]).wait()

    w = (route_ref[:, 4:5], route_ref[:, 5:6])
    for k in range(2):
        _halves_from_token_major(f32_ref.at[k], ybuf_ref[slot, k].astype(F32))
    for c in range(n_row):
        sl = slice(c * LANES, (c + 1) * LANES)
        y0 = _load_chunk(f32_ref.at[0], tm, c)
        y1 = _load_chunk(f32_ref.at[1], tm, c)
        x2_ref[:, sl] = x1_ref[:, sl] + w[0] * y0 + w[1] * y1
    x2 = x2_ref[...]
    gate = jax.nn.sigmoid(jnp.dot(_rms(x2, gin_ref[...]).astype(BF16), wg_ref[...],
                                  preferred_element_type=F32))
    e = jnp.dot(p_ref[...].astype(BF16), wp_ref[...], preferred_element_type=F32) * gate
    o_ref[...] = x2 + _rms(e, gout_ref[...])


def _ple(x1, ys, pos, route, p, g_in, w_gate, w_proj, g_out):
    t, d = x1.shape
    tm = min(TM_ROW, t)
    n_steps = t // tm
    pd = p.shape[1]
    pos3 = pos.reshape(n_steps, 1, 2 * tm)
    row_spec = pl.BlockSpec((tm, d), lambda i: (i, 0))
    const = lambda shape: pl.BlockSpec(shape, lambda i: (0,) * len(shape))
    return pl.pallas_call(
        _ple_kernel,
        grid=(n_steps,),
        in_specs=[
            pl.BlockSpec((1, 1, 2 * tm), lambda i: (i, 0, 0), memory_space=pltpu.SMEM),
            pl.BlockSpec((1, 1, 2 * tm), lambda i: (jnp.minimum(i + 1, n_steps - 1), 0, 0),
                         memory_space=pltpu.SMEM),
            row_spec,
            pl.BlockSpec((tm, LANES), lambda i: (i, 0)),
            pl.BlockSpec((tm, pd), lambda i: (i, 0)),
            const((1, d)), const((d, d)), const((pd, d)), const((1, d)),
            pl.BlockSpec(memory_space=pl.ANY),
        ],
        out_specs=row_spec,
        out_shape=jax.ShapeDtypeStruct((t, d), F32),
        scratch_shapes=[
            pltpu.VMEM((2, 2, tm * ROW_TILES, LANES), BF16),
            pltpu.VMEM((2, 2, tm * HALF_TILES, LANES), F32),
            pltpu.VMEM((tm, d), F32),
            pltpu.SemaphoreType.DMA((2,)),
        ],
        compiler_params=_cparams("arbitrary"),
        name="moe_combine_ple",
    )(pos3, pos3, x1, route, p, g_in, w_gate, w_proj, g_out, ys)


def _router_weights(w_group, b_group, w_expert, b_expert):
    d = w_group.shape[0]
    pad = LANES - N_EXPERTS - N_GROUPS
    w = jnp.concatenate([w_expert, w_group, jnp.zeros((d, pad), F32)], axis=1)
    b = jnp.concatenate([b_expert, b_group, jnp.zeros((pad,), F32)]).reshape(1, LANES)
    w_hi = w.astype(BF16)
    w_lo = (w - w_hi.astype(F32)).astype(BF16)
    return jnp.concatenate([w_hi, w_hi, w_lo], axis=0), b


def _sorted_layout(route, counts, t):
    tm = TM_EXPERT
    n_tiles = (2 * t + N_EXPERTS * (tm - 1)) // tm + 1
    ids = route[:, 0:4].T.astype(jnp.int32)
    eid, rank = ids[0:2], ids[2:4]
    cnt = counts[0, :N_EXPERTS].astype(jnp.int32)
    padded = ((cnt + tm - 1) // tm) * tm
    ends = jnp.cumsum(padded)
    offs = ends - padded
    pos = (offs[eid] + rank).reshape(2, t // min(TM_ROW, t), -1).transpose(1, 0, 2)
    pos = pos.reshape(pos.shape[0], 1, -1)
    tile_start = jnp.arange(n_tiles, dtype=jnp.int32) * tm
    n_used = (ends[-1] // tm).astype(jnp.int32)
    probe = jnp.minimum(tile_start, ends[-1] - tm)
    te = jnp.sum((probe[:, None] >= ends[None, :]).astype(jnp.int32), axis=1)
    return dict(pos=pos, tile_expert=te, n_used=n_used.reshape(1), pad_start=offs + cnt,
                pad_n=padded - cnt, n_sorted=n_tiles * tm)


def _moe_and_ple(x, mix_out_kwargs, w_out, g_ffn, router, expert_w, p_i, ple):
    t = x.shape[0]
    x1, route, counts = _mix_out(x, w_out, g_ffn, *router, **mix_out_kwargs)
    lay = _sorted_layout(route, counts, t)
    xs = _dispatch(x1, g_ffn, lay["pos"], lay["pad_start"], lay["pad_n"], lay["n_used"], lay["n_sorted"])
    ys = _experts(xs, *expert_w, lay["tile_expert"], lay["n_used"])
    return _ple(x1, ys, lay["pos"], route, p_i, *ple)


def kernel(x, p, norm_mix, norm_ffn, sb_w_in, sb_q_norm, sb_k_norm, sb_w_out, sg_w_in, sg_v_norm, sg_w_s, sg_b_s, sg_w_out, moe_w_group, moe_b_group, moe_w_expert, moe_b_expert, moe_w_gate, moe_w_up, moe_w_down, ple_norm_in, ple_w_gate, ple_w_proj, ple_norm_out):
    b, s, d = x.shape
    depth = norm_mix.shape[0]
    heads = d // LANES
    assert d == ROW_TILES * LANES
    t = b * s
    xt = x.reshape(t, d)
    row = lambda v: v.reshape(1, -1)
    for i in range(depth):
        j = i // 2
        router = _router_weights(moe_w_group[i], moe_b_group[i], moe_w_expert[i], moe_b_expert[i])
        expert_w = (i, moe_w_gate, moe_w_up, moe_w_down)
        ple = (row(ple_norm_in[i]), ple_w_gate[i].astype(BF16), ple_w_proj[i].astype(BF16), row(ple_norm_out[i]))
        if i % 2 == 0:
            q_gain = sb_q_norm[j] * (LANES ** -0.5 * LOG2E)
            colgain = jnp.concatenate([jnp.tile(q_gain, heads), jnp.tile(sb_k_norm[j], heads),
                                       jnp.ones((d,), F32)]).reshape(1, 3 * d)
            qkv = _qkv_proj(xt, row(norm_mix[i]), sb_w_in[j].astype(BF16), colgain)
            o = _stick_breaking(qkv.reshape(b, s, 3 * d), heads).reshape(t, d)
            mix = dict(attn_out=o)
            w_out = sb_w_out[j]
        else:
            z = _sg_in_proj(xt, row(norm_mix[i]), sg_w_in[j].astype(BF16), row(sg_v_norm[j]))
            mix = dict(sg=(z, sg_w_s[j], sg_b_s[j].T))
            w_out = sg_w_out[j]
        xt = _moe_and_ple(xt, mix, w_out.astype(BF16), row(norm_ffn[i]), router, expert_w,
                          p[i].reshape(t, -1), ple)
    return xt.reshape(b, s, d)
```

```python
import functools

import jax
import jax.numpy as jnp
from jax import lax
from jax.experimental import pallas as pl
from jax.experimental.pallas import tpu as pltpu

F32 = jnp.float32
BF16 = jnp.bfloat16

LANES = 128
ROW_TILES = 16
HALF_TILES = ROW_TILES // 2
VMEM_LIMIT_BYTES = 56 * 1024 * 1024
EPS = 1e-6
LOG2E = 1.4426950408889634
INV_LN2 = LOG2E
SIGN_BIT = -2 ** 31

N_GROUPS = 4
EXPERTS_PER_GROUP = 8
N_EXPERTS = N_GROUPS * EXPERTS_PER_GROUP
ROUTER_GROUP_LANE0 = N_EXPERTS

TM_PROJ = 512
TM_ROW = 256
TQ = 256
HEADS_PER_STEP = 4
TM_EXPERT = 256


def _cparams(*sem):
    return pltpu.CompilerParams(dimension_semantics=sem, vmem_limit_bytes=VMEM_LIMIT_BYTES)


def _rms(x, g):
    ms = jnp.mean(x * x, axis=-1, keepdims=True)
    return x * lax.rsqrt(ms + EPS) * g


def _qkv_kernel(x_ref, g_ref, w_ref, cg_ref, o_ref, xn_ref, *, n_norm_tiles):
    j = pl.program_id(1)

    @pl.when(j == 0)
    def _():
        xn_ref[...] = _rms(x_ref[...], g_ref[...]).astype(BF16)

    is_norm = j < n_norm_tiles
    acc = jnp.dot(xn_ref[...], w_ref[...], preferred_element_type=F32)
    for h in range(acc.shape[1] // LANES):
        sl = slice(h * LANES, (h + 1) * LANES)
        a = acc[:, sl]
        ms = jnp.mean(a * a, axis=-1, keepdims=True)
        scale = jnp.where(is_norm, lax.rsqrt(ms + EPS), 1.0)
        o_ref[:, sl] = (a * scale * cg_ref[:, sl]).astype(BF16)


def _qkv_proj(x, g, w, colgain, *, tn=1024):
    t, d = x.shape
    n = w.shape[1]
    tm = min(TM_PROJ, t)
    tn = min(tn, d)
    return pl.pallas_call(
        functools.partial(_qkv_kernel, n_norm_tiles=2 * d // tn),
        grid=(t // tm, n // tn),
        in_specs=[
            pl.BlockSpec((tm, d), lambda i, j: (i, 0)),
            pl.BlockSpec((1, d), lambda i, j: (0, 0)),
            pl.BlockSpec((d, tn), lambda i, j: (0, j)),
            pl.BlockSpec((1, tn), lambda i, j: (0, j)),
        ],
        out_specs=pl.BlockSpec((tm, tn), lambda i, j: (i, j)),
        out_shape=jax.ShapeDtypeStruct((t, n), BF16),
        scratch_shapes=[pltpu.VMEM((tm, d), BF16)],
        compiler_params=_cparams("parallel", "arbitrary"),
        name="qkv_proj",
    )(x, g, w, colgain)


def _sg_in_kernel(x_ref, g_ref, w_ref, vg_ref, o_ref, xn_ref):
    j = pl.program_id(1)

    @pl.when(j == 0)
    def _():
        xn_ref[...] = _rms(x_ref[...], g_ref[...]).astype(BF16)

    z = jax.nn.gelu(jnp.dot(xn_ref[...], w_ref[...], preferred_element_type=F32))
    is_v = j == 1
    ms = jnp.mean(z * z, axis=-1, keepdims=True)
    scale = jnp.where(is_v, lax.rsqrt(ms + EPS), 1.0)
    gain = jnp.where(is_v, vg_ref[...], 1.0)
    o_ref[...] = (z * scale * gain).astype(BF16)


def _sg_in_proj(x, g, w, v_gain):
    t, d = x.shape
    tm = min(TM_PROJ, t)
    return pl.pallas_call(
        _sg_in_kernel,
        grid=(t // tm, 2),
        in_specs=[
            pl.BlockSpec((tm, d), lambda i, j: (i, 0)),
            pl.BlockSpec((1, d), lambda i, j: (0, 0)),
            pl.BlockSpec((d, d), lambda i, j: (0, j)),
            pl.BlockSpec((1, d), lambda i, j: (0, 0)),
        ],
        out_specs=pl.BlockSpec((tm, d), lambda i, j: (i, j)),
        out_shape=jax.ShapeDtypeStruct((t, 2 * d), BF16),
        scratch_shapes=[pltpu.VMEM((tm, d), BF16)],
        compiler_params=_cparams("parallel", "arbitrary"),
        name="sg_in_proj",
    )(x, g, w, v_gain)


def _attn_kernel(q_ref, k_ref, v_ref, uu_ref, o_ref, acc_ref, carry_ref, za_ref, zb_ref):
    qi = pl.program_id(2)
    tq = q_ref.shape[0]
    tk = uu_ref.shape[1]
    n_heads = q_ref.shape[1] // LANES
    heads = range(n_heads)
    head = lambda ref_or_val, h: ref_or_val[:, h * LANES:(h + 1) * LANES]
    qs = [head(q_ref, h) for h in heads]

    def key_rows(kj):
        return pl.ds(pl.multiple_of(kj * tk, tk), tk)

    def store_logits(kj, z_out):
        k_all = k_ref[key_rows(kj), :]
        for h in heads:
            z_out[h] = lax.dot_general(qs[h], head(k_all, h), (((1,), (1,)), ((), ())),
                                       preferred_element_type=F32)

    def neg_abs(z):
        return lax.bitcast_convert_type(lax.bitcast_convert_type(z, jnp.int32) | SIGN_BIT, F32)

    def tile(kj, z_in, z_out, causal):
        store_logits(jnp.maximum(kj - 1, 0), z_out)
        v_all = v_ref[key_rows(kj), :]
        cs = []
        for h in heads:
            z = z_in[h]
            sp = jnp.maximum(z, 0.0) + jnp.log(1.0 + jnp.exp2(neg_abs(z))) * INV_LN2
            if causal is not None:
                sp = jnp.where(causal, sp, 0.0)
            cs.append(jnp.dot(sp.astype(BF16), uu_ref[...], preferred_element_type=F32))
        for h in heads:
            w = jnp.exp2(z_in[h] - cs[h])
            if causal is not None:
                w = jnp.where(causal, w, 0.0)
            pv = jnp.dot(w.astype(BF16), head(v_all, h), preferred_element_type=F32)
            carry = carry_ref[h]
            acc_ref[h] += jnp.exp2(-carry) * pv
            carry_ref[h] = carry + cs[h][:, 0:1]

    acc_ref[...] = jnp.zeros_like(acc_ref)
    carry_ref[...] = jnp.zeros_like(carry_ref)
    rows = lax.broadcasted_iota(jnp.int32, (tq, tk), 0)
    cols = lax.broadcasted_iota(jnp.int32, (tq, tk), 1)
    store_logits(qi, za_ref)
    tile(qi, za_ref, zb_ref, cols < rows)

    def pair(kj):
        tile(kj, zb_ref, za_ref, None)
        tile(kj - 1, za_ref, zb_ref, None)

    def quad(p, c):
        kj = qi - 1 - 4 * p
        pair(kj)
        pair(kj - 2)
        return c

    lax.fori_loop(0, qi // 4, quad, 0)
    left = qi % 4

    @pl.when(left >= 2)
    def _():
        pair(left - 1)

    @pl.when(left % 2 == 1)
    def _():
        tile(0, zb_ref, za_ref, None)

    for h in heads:
        o_ref[:, h * LANES:(h + 1) * LANES] = acc_ref[h].astype(o_ref.dtype)


def _stick_breaking(qkv, heads):
    b, s, d3 = qkv.shape
    d = d3 // 3
    tq = min(TQ, s)
    tk = tq
    hw = HEADS_PER_STEP * LANES
    groups = heads // HEADS_PER_STEP
    j = lax.broadcasted_iota(jnp.int32, (tk, tk), 0)
    c = lax.broadcasted_iota(jnp.int32, (tk, tk), 1)
    uu = (j >= c).astype(BF16)
    return pl.pallas_call(
        _attn_kernel,
        grid=(b, groups, s // tq),
        in_specs=[
            pl.BlockSpec((None, tq, hw), lambda bi, h, i: (bi, i, h)),
            pl.BlockSpec((None, s, hw), lambda bi, h, i: (bi, 0, groups + h)),
            pl.BlockSpec((None, s, hw), lambda bi, h, i: (bi, 0, 2 * groups + h)),
            pl.BlockSpec((tk, tk), lambda bi, h, i: (0, 0)),
        ],
        out_specs=pl.BlockSpec((None, tq, hw), lambda bi, h, i: (bi, i, h)),
        out_shape=jax.ShapeDtypeStruct((b, s, d), BF16),
        scratch_shapes=[pltpu.VMEM((HEADS_PER_STEP, tq, LANES), F32),
                        pltpu.VMEM((HEADS_PER_STEP, tq, LANES), F32),
                        pltpu.VMEM((HEADS_PER_STEP, tq, tk), F32),
                        pltpu.VMEM((HEADS_PER_STEP, tq, tk), F32)],
        compiler_params=_cparams("parallel", "parallel", "arbitrary"),
        name="stick_breaking_attention",
    )(qkv, qkv, qkv, uu)


def _split_hi_lo(x):
    hi = x.astype(BF16)
    return hi, (x - hi.astype(F32)).astype(BF16)


def _route_tile(h2, wr_ref, br_ref, ltri_ref, cnt_ref):
    tm = h2.shape[0]
    hi, lo = _split_hi_lo(h2)
    lhs = jnp.concatenate([hi, lo, hi], axis=1)
    logits = jnp.dot(lhs, wr_ref[...], preferred_element_type=F32) + br_ref[...]

    lane = lax.broadcasted_iota(jnp.int32, (tm, LANES), 1)
    neg = jnp.float32(-jnp.inf)
    big = jnp.int32(LANES)

    is_group = (lane >= ROUTER_GROUP_LANE0) & (lane < ROUTER_GROUP_LANE0 + N_GROUPS)
    gl = jnp.where(is_group, logits, neg)
    gmax = jnp.max(gl, axis=-1, keepdims=True)
    g_idx = jnp.min(jnp.where(gl == gmax, lane - ROUTER_GROUP_LANE0, big), axis=-1, keepdims=True)
    g_w = 1.0 / jnp.sum(jnp.where(is_group, jnp.exp(gl - gmax), 0.0), axis=-1, keepdims=True)

    in_group = (lane < N_EXPERTS) & ((lane // EXPERTS_PER_GROUP) == g_idx)
    el = jnp.where(in_group, logits, neg)
    v1 = jnp.max(el, axis=-1, keepdims=True)
    i1 = jnp.min(jnp.where(el == v1, lane, big), axis=-1, keepdims=True)
    el2 = jnp.where(lane == i1, neg, el)
    v2 = jnp.max(el2, axis=-1, keepdims=True)
    i2 = jnp.min(jnp.where(el2 == v2, lane, big), axis=-1, keepdims=True)
    e21 = jnp.exp(v2 - v1)
    den = 1.0 + e21
    w1 = g_w * (1.0 / den)
    w2 = g_w * (e21 / den)

    oh1 = lane == i1
    oh2 = lane == i2
    onehot = (oh1 | oh2).astype(BF16)
    ahead = jnp.dot(ltri_ref[...], onehot, preferred_element_type=F32) + cnt_ref[...]
    r1 = jnp.sum(jnp.where(oh1, ahead, 0.0), axis=-1, keepdims=True)
    r2 = jnp.sum(jnp.where(oh2, ahead, 0.0), axis=-1, keepdims=True)
    cnt_ref[...] += jnp.sum(onehot.astype(F32), axis=0, keepdims=True)

    rec = jnp.where(lane == 0, i1.astype(F32), 0.0)
    rec = jnp.where(lane == 1, i2.astype(F32), rec)
    rec = jnp.where(lane == 2, r1, rec)
    rec = jnp.where(lane == 3, r2, rec)
    rec = jnp.where(lane == 4, w1, rec)
    rec = jnp.where(lane == 5, w2, rec)
    return rec


def _store_chunks(ref, val):
    tm = val.shape[0]
    for c in range(ROW_TILES):
        ref.at[c // HALF_TILES][pl.ds(c % HALF_TILES, tm, stride=HALF_TILES), :] = val[:, c * LANES:(c + 1) * LANES]


def _load_chunk(ref, tm, c):
    return ref.at[c // HALF_TILES][pl.ds(c % HALF_TILES, tm, stride=HALF_TILES), :]


def _halves_from_token_major(ref, val):
    tm = val.shape[0] // ROW_TILES
    v4 = val.reshape(tm, 2, HALF_TILES, LANES)
    for j in range(2):
        ref[j] = v4[:, j].reshape(tm * HALF_TILES, LANES)


def _token_major_from_halves(ref):
    tm = ref.shape[1] // HALF_TILES
    parts = [ref[j].reshape(tm, 1, HALF_TILES, LANES) for j in range(2)]
    return jnp.concatenate(parts, axis=1).reshape(tm * ROW_TILES, LANES)


def _mix_out_epilogue(a, x_ref, wo_ref, gf_ref, wr_ref, br_ref, x1_ref, route_ref, cnt_out_ref,
                      ltri_ref, cnt_ref):
    tm = a.shape[0]

    @pl.when(pl.program_id(0) == 0)
    def _():
        r = lax.broadcasted_iota(jnp.int32, (tm, tm), 0)
        c = lax.broadcasted_iota(jnp.int32, (tm, tm), 1)
        ltri_ref[...] = (c < r).astype(BF16)
        cnt_ref[...] = jnp.zeros_like(cnt_ref)

    x1 = x_ref[...] + jnp.dot(a, wo_ref[...], preferred_element_type=F32)
    x1_ref[...] = x1
    route_ref[...] = _route_tile(_rms(x1, gf_ref[...]), wr_ref, br_ref, ltri_ref, cnt_ref)
    cnt_out_ref[...] = cnt_ref[...]


def _attn_out_kernel(a_ref, x_ref, wo_ref, gf_ref, wr_ref, br_ref,
                     x1_ref, route_ref, cnt_out_ref, ltri_ref, cnt_ref):
    _mix_out_epilogue(a_ref[...], x_ref, wo_ref, gf_ref, wr_ref, br_ref,
                      x1_ref, route_ref, cnt_out_ref, ltri_ref, cnt_ref)


def _sg_out_kernel(u_ref, vn_ref, ws_ref, bs_ref, x_ref, wo_ref, gf_ref, wr_ref, br_ref,
                   x1_ref, route_ref, cnt_out_ref, ltri_ref, cnt_ref, a_ref):
    tm, d = u_ref.shape
    r = lax.broadcasted_iota(jnp.int32, (LANES, LANES), 0)
    c = lax.broadcasted_iota(jnp.int32, (LANES, LANES), 1)
    keep = c <= r
    for g in range(d // LANES):
        sl = slice(g * LANES, (g + 1) * LANES)
        wc = jnp.where(keep, ws_ref[g], 0.0).astype(BF16)
        bias = bs_ref[:, g:g + 1]
        for ch in range(tm // LANES):
            rs = slice(ch * LANES, (ch + 1) * LANES)
            mixed = jnp.dot(wc, vn_ref[rs, sl], preferred_element_type=F32) + bias
            a_ref[rs, sl] = (u_ref[rs, sl].astype(F32) * mixed).astype(BF16)
    _mix_out_epilogue(a_ref[...], x_ref, wo_ref, gf_ref, wr_ref, br_ref,
                      x1_ref, route_ref, cnt_out_ref, ltri_ref, cnt_ref)


def _mix_out(x, w_out, g_ffn, w_router, b_router, *, attn_out=None, sg=None):
    t, d = x.shape
    tm = min(TM_ROW, t)
    row_spec = pl.BlockSpec((tm, d), lambda i: (i, 0))
    const = lambda shape: pl.BlockSpec(shape, lambda i: (0,) * len(shape))
    common_in = [row_spec, const((d, d)), const((1, d)), const(w_router.shape), const((1, LANES))]
    common_args = (x, w_out, g_ffn, w_router, b_router)
    out_specs = [row_spec, pl.BlockSpec((tm, LANES), lambda i: (i, 0)), const((1, LANES))]
    out_shape = [
        jax.ShapeDtypeStruct((t, d), F32),
        jax.ShapeDtypeStruct((t, LANES), F32),
        jax.ShapeDtypeStruct((1, LANES), F32),
    ]
    scratch = [pltpu.VMEM((tm, tm), BF16), pltpu.VMEM((1, LANES), F32)]
    if attn_out is not None:
        return pl.pallas_call(
            _attn_out_kernel,
            grid=(t // tm,),
            in_specs=[row_spec] + common_in,
            out_specs=out_specs, out_shape=out_shape, scratch_shapes=scratch,
            compiler_params=_cparams("arbitrary"),
            name="attn_out_router",
        )(attn_out, *common_args)
    z, w_s, b_s_t = sg
    return pl.pallas_call(
        _sg_out_kernel,
        grid=(t // tm,),
        in_specs=[
            pl.BlockSpec((tm, d), lambda i: (i, 0)),
            pl.BlockSpec((tm, d), lambda i: (i, 1)),
            const(w_s.shape),
            const(b_s_t.shape),
        ] + common_in,
        out_specs=out_specs, out_shape=out_shape,
        scratch_shapes=scratch + [pltpu.VMEM((tm, d), BF16)],
        compiler_params=_cparams("arbitrary"),
        name="sg_out_router",
    )(z, z, w_s, b_s_t, *common_args)


def _row_slice(ref, row):
    return ref.at[pl.ds(pl.multiple_of(row * ROW_TILES, ROW_TILES), ROW_TILES)]


def _for_each_token(n, fn, unroll=8):
    def group(o, carry):
        for u in range(unroll):
            fn(o * unroll + u)
        return carry
    lax.fori_loop(0, n // unroll, group, 0)


def _dispatch_kernel(pos_ref, pad_start_ref, pad_n_ref, n_used_ref, x1_ref, g_ref, xs_ref,
                     f32_ref, stage_ref, zero_ref, sems, pad_sem):
    i = pl.program_id(0)
    n_steps = pl.num_programs(0)
    td = x1_ref.shape[0]
    slot = i % 2

    def wait_slot(s):
        for _ in range(2):
            pltpu.make_async_copy(stage_ref.at[s], xs_ref.at[pl.ds(0, td * ROW_TILES)], sems.at[s]).wait()

    @pl.when(i == 0)
    def _():
        zero_ref[...] = jnp.zeros_like(zero_ref)
        tile_rows = zero_ref.shape[0]
        n_tiles = xs_ref.shape[0] // tile_rows

        def pad_copy(e, r):
            return pltpu.make_async_copy(zero_ref.at[pl.ds(0, ROW_TILES)],
                                         _row_slice(xs_ref, pad_start_ref[e] + r), pad_sem)

        def tile_copy(j):
            return pltpu.make_async_copy(
                zero_ref, xs_ref.at[pl.ds(pl.multiple_of(j * tile_rows, tile_rows), tile_rows)], pad_sem)

        def start_all(e, carry):
            lax.fori_loop(0, pad_n_ref[e], lambda r, c: (pad_copy(e, r).start(), c)[1], 0)
            return carry

        def wait_all(e, carry):
            lax.fori_loop(0, pad_n_ref[e], lambda r, c: (pad_copy(e, r).wait(), c)[1], 0)
            return carry

        lax.fori_loop(0, pad_n_ref.shape[0], start_all, 0)
        lax.fori_loop(n_used_ref[0], n_tiles, lambda j, c: (tile_copy(j).start(), c)[1], 0)
        lax.fori_loop(0, pad_n_ref.shape[0], wait_all, 0)
        lax.fori_loop(n_used_ref[0], n_tiles, lambda j, c: (tile_copy(j).wait(), c)[1], 0)

    @pl.when(i >= 2)
    def _():
        wait_slot(slot)

    _store_chunks(f32_ref, _rms(x1_ref[...], g_ref[...]))
    stage_ref[slot] = _token_major_from_halves(f32_ref).astype(BF16)

    def issue(t):
        src = stage_ref.at[slot, pl.ds(pl.multiple_of(t * ROW_TILES, ROW_TILES), ROW_TILES)]
        for k in range(2):
            pltpu.make_async_copy(src, _row_slice(xs_ref, pos_ref[0, 0, k * td + t]), sems.at[slot]).start()

    _for_each_token(td, issue)

    @pl.when(i == n_steps - 1)
    def _():
        wait_slot(slot)

    @pl.when((i == n_steps - 1) & (i >= 1))
    def _():
        wait_slot(1 - slot)


def _dispatch(x1, g_ffn, pos, pad_start, pad_n, n_used, n_sorted_rows):
    t, d = x1.shape
    td = min(TM_ROW, t)
    pos3 = pos.reshape(t // td, 1, 2 * td)
    smem = pl.BlockSpec(memory_space=pltpu.SMEM)
    return pl.pallas_call(
        _dispatch_kernel,
        grid=(t // td,),
        in_specs=[
            pl.BlockSpec((1, 1, 2 * td), lambda i: (i, 0, 0), memory_space=pltpu.SMEM),
            smem, smem, smem,
            pl.BlockSpec((td, d), lambda i: (i, 0)),
            pl.BlockSpec((1, d), lambda i: (0, 0)),
        ],
        out_specs=pl.BlockSpec(memory_space=pl.ANY),
        out_shape=jax.ShapeDtypeStruct((n_sorted_rows * ROW_TILES, LANES), BF16),
        scratch_shapes=[
            pltpu.VMEM((2, td * HALF_TILES, LANES), F32),
            pltpu.VMEM((2, td * ROW_TILES, LANES), BF16),
            pltpu.VMEM((TM_EXPERT * ROW_TILES, LANES), BF16),
            pltpu.SemaphoreType.DMA((2,)),
            pltpu.SemaphoreType.DMA(()),
        ],
        compiler_params=_cparams("arbitrary"),
        name="moe_dispatch",
    )(pos3, pad_start, pad_n, n_used, x1, g_ffn)


def _expert_kernel(tile_expert_ref, n_used_ref, xs_ref, wg_ref, wu_ref, wd_ref, ys_ref,
                   f32_ref, a_ref, wgu_bf_ref, wd_bf_ref):
    i = pl.program_id(0)
    tm, d = a_ref.shape
    n_row = d // LANES
    f = wd_ref.shape[0]

    @pl.when((i == 0) | (tile_expert_ref[i] != tile_expert_ref[jnp.maximum(i - 1, 0)]))
    def _():
        wgu_bf_ref[:, :f] = wg_ref[...].astype(BF16)
        wgu_bf_ref[:, f:] = wu_ref[...].astype(BF16)
        wd_bf_ref[...] = wd_ref[...].astype(BF16)

    @pl.when(i < n_used_ref[0])
    def _():
        _halves_from_token_major(f32_ref, xs_ref[...].astype(F32))
        for c in range(n_row):
            a_ref[:, c * LANES:(c + 1) * LANES] = _load_chunk(f32_ref, tm, c).astype(BF16)
        gu = jnp.dot(a_ref[...], wgu_bf_ref[...], preferred_element_type=F32)
        hidden = (jax.nn.silu(gu[:, :f]) * gu[:, f:]).astype(BF16)
        _store_chunks(f32_ref, jnp.dot(hidden, wd_bf_ref[...], preferred_element_type=F32))
        ys_ref[...] = _token_major_from_halves(f32_ref).astype(BF16)

    @pl.when(i >= n_used_ref[0])
    def _():
        ys_ref[...] = jnp.zeros_like(ys_ref)


def _experts(xs, layer, w_gate, w_up, w_down, tile_expert, n_used):
    _, e, d, f = w_gate.shape
    n_rows = xs.shape[0] // ROW_TILES
    tm = TM_EXPERT
    n_tiles = n_rows // tm
    return pl.pallas_call(
        _expert_kernel,
        grid_spec=pltpu.PrefetchScalarGridSpec(
            num_scalar_prefetch=2,
            grid=(n_tiles,),
            in_specs=[
                pl.BlockSpec((tm * ROW_TILES, LANES), lambda i, te, nu: (jnp.minimum(i, nu[0] - 1), 0)),
                pl.BlockSpec((None, None, d, f), lambda i, te, nu: (layer, te[i], 0, 0)),
                pl.BlockSpec((None, None, d, f), lambda i, te, nu: (layer, te[i], 0, 0)),
                pl.BlockSpec((None, None, f, d), lambda i, te, nu: (layer, te[i], 0, 0)),
            ],
            out_specs=pl.BlockSpec((tm * ROW_TILES, LANES), lambda i, te, nu: (i, 0)),
            scratch_shapes=[pltpu.VMEM((2, tm * HALF_TILES, LANES), F32), pltpu.VMEM((tm, d), BF16),
                            pltpu.VMEM((d, 2 * f), BF16), pltpu.VMEM((f, d), BF16)],
        ),
        out_shape=jax.ShapeDtypeStruct(xs.shape, BF16),
        compiler_params=_cparams("arbitrary"),
        name="moe_experts",
    )(tile_expert, n_used, xs, w_gate, w_up, w_down)


def _ple_kernel(pos_ref, pos_next_ref, x1_ref, route_ref, p_ref, gin_ref, wg_ref, wp_ref, gout_ref, ys_ref,
                o_ref, ybuf_ref, f32_ref, x2_ref, sems):
    i = pl.program_id(0)
    n_steps = pl.num_programs(0)
    tm, d = x1_ref.shape
    n_row = d // LANES
    slot = i % 2

    def gather(p_ref_, s):
        def issue(t):
            for k in range(2):
                dst = ybuf_ref.at[s, k, pl.ds(pl.multiple_of(t * ROW_TILES, ROW_TILES), ROW_TILES)]
                pltpu.make_async_copy(_row_slice(ys_ref, p_ref_[0, 0, k * tm + t]), dst, sems.at[s]).start()
        _for_each_token(tm, issue)

    @pl.when(i == 0)
    def _():
        gather(pos_ref, 0)

    @pl.when(i + 1 < n_steps)
    def _():
        gather(pos_next_ref, 1 - slot)

    for k in range(2):
        pltpu.make_async_copy(ys_ref.at[pl.ds(0, tm * ROW_TILES)], ybuf_ref.at[slot, k], sems.at[slot]).wait()

    w = (route_ref[:, 4:5], route_ref[:, 5:6])
    for k in range(2):
        _halves_from_token_major(f32_ref.at[k], ybuf_ref[slot, k].astype(F32))
    for c in range(n_row):
        sl = slice(c * LANES, (c + 1) * LANES)
        y0 = _load_chunk(f32_ref.at[0], tm, c)
        y1 = _load_chunk(f32_ref.at[1], tm, c)
        x2_ref[:, sl] = x1_ref[:, sl] + w[0] * y0 + w[1] * y1
    x2 = x2_ref[...]
    gate = jax.nn.sigmoid(jnp.dot(_rms(x2, gin_ref[...]).astype(BF16), wg_ref[...],
                                  preferred_element_type=F32))
    e = jnp.dot(p_ref[...].astype(BF16), wp_ref[...], preferred_element_type=F32) * gate
    o_ref[...] = x2 + _rms(e, gout_ref[...])


def _ple(x1, ys, pos, route, p, g_in, w_gate, w_proj, g_out):
    t, d = x1.shape
    tm = min(TM_ROW, t)
    n_steps = t // tm
    pd = p.shape[1]
    pos3 = pos.reshape(n_steps, 1, 2 * tm)
    row_spec = pl.BlockSpec((tm, d), lambda i: (i, 0))
    const = lambda shape: pl.BlockSpec(shape, lambda i: (0,) * len(shape))
    return pl.pallas_call(
        _ple_kernel,
        grid=(n_steps,),
        in_specs=[
            pl.BlockSpec((1, 1, 2 * tm), lambda i: (i, 0, 0), memory_space=pltpu.SMEM),
            pl.BlockSpec((1, 1, 2 * tm), lambda i: (jnp.minimum(i + 1, n_steps - 1), 0, 0),
                         memory_space=pltpu.SMEM),
            row_spec,
            pl.BlockSpec((tm, LANES), lambda i: (i, 0)),
            pl.BlockSpec((tm, pd), lambda i: (i, 0)),
            const((1, d)), const((d, d)), const((pd, d)), const((1, d)),
            pl.BlockSpec(memory_space=pl.ANY),
        ],
        out_specs=row_spec,
        out_shape=jax.ShapeDtypeStruct((t, d), F32),
        scratch_shapes=[
            pltpu.VMEM((2, 2, tm * ROW_TILES, LANES), BF16),
            pltpu.VMEM((2, 2, tm * HALF_TILES, LANES), F32),
            pltpu.VMEM((tm, d), F32),
            pltpu.SemaphoreType.DMA((2,)),
        ],
        compiler_params=_cparams("arbitrary"),
        name="moe_combine_ple",
    )(pos3, pos3, x1, route, p, g_in, w_gate, w_proj, g_out, ys)


def _router_weights(w_group, b_group, w_expert, b_expert):
    d = w_group.shape[0]
    pad = LANES - N_EXPERTS - N_GROUPS
    w = jnp.concatenate([w_expert, w_group, jnp.zeros((d, pad), F32)], axis=1)
    b = jnp.concatenate([b_expert, b_group, jnp.zeros((pad,), F32)]).reshape(1, LANES)
    w_hi = w.astype(BF16)
    w_lo = (w - w_hi.astype(F32)).astype(BF16)
    return jnp.concatenate([w_hi, w_hi, w_lo], axis=0), b


def _sorted_layout(route, counts, t):
    tm = TM_EXPERT
    n_tiles = (2 * t + N_EXPERTS * (tm - 1)) // tm + 1
    ids = route[:, 0:4].T.astype(jnp.int32)
    eid, rank = ids[0:2], ids[2:4]
    cnt = counts[0, :N_EXPERTS].astype(jnp.int32)
    padded = ((cnt + tm - 1) // tm) * tm
    ends = jnp.cumsum(padded)
    offs = ends - padded
    one_hot = eid[None] == jnp.arange(N_EXPERTS, dtype=jnp.int32)[:, None, None]
    pos = rank + jnp.sum(jnp.where(one_hot, offs[:, None, None], 0), axis=0)
    pos = pos.reshape(2, t // min(TM_ROW, t), -1).transpose(1, 0, 2)
    pos = pos.reshape(pos.shape[0], 1, -1)
    tile_start = jnp.arange(n_tiles, dtype=jnp.int32) * tm
    n_used = (ends[-1] // tm).astype(jnp.int32)
    probe = jnp.minimum(tile_start, ends[-1] - tm)
    te = jnp.sum((probe[:, None] >= ends[None, :]).astype(jnp.int32), axis=1)
    return dict(pos=pos, tile_expert=te, n_used=n_used.reshape(1), pad_start=offs + cnt,
                pad_n=padded - cnt, n_sorted=n_tiles * tm)


def _moe_and_ple(x, mix_out_kwargs, w_out, g_ffn, router, expert_w, p_i, ple):
    t = x.shape[0]
    x1, route, counts = _mix_out(x, w_out, g_ffn, *router, **mix_out_kwargs)
    lay = _sorted_layout(route, counts, t)
    xs = _dispatch(x1, g_ffn, lay["pos"], lay["pad_start"], lay["pad_n"], lay["n_used"], lay["n_sorted"])
    ys = _experts(xs, *expert_w, lay["tile_expert"], lay["n_used"])
    return _ple(x1, ys, lay["pos"], route, p_i, *ple)


def kernel(x, p, norm_mix, norm_ffn, sb_w_in, sb_q_norm, sb_k_norm, sb_w_out, sg_w_in, sg_v_norm, sg_w_s, sg_b_s, sg_w_out, moe_w_group, moe_b_group, moe_w_expert, moe_b_expert, moe_w_gate, moe_w_up, moe_w_down, ple_norm_in, ple_w_gate, ple_w_proj, ple_norm_out):
    b, s, d = x.shape
    depth = norm_mix.shape[0]
    heads = d // LANES
    assert d == ROW_TILES * LANES
    t = b * s
    xt = x.reshape(t, d)
    row = lambda v: v.reshape(1, -1)
    for i in range(depth):
        j = i // 2
        router = _router_weights(moe_w_group[i], moe_b_group[i], moe_w_expert[i], moe_b_expert[i])
        expert_w = (i, moe_w_gate, moe_w_up, moe_w_down)
        ple = (row(ple_norm_in[i]), ple_w_gate[i].astype(BF16), ple_w_proj[i].astype(BF16), row(ple_norm_out[i]))
        if i % 2 == 0:
            q_gain = sb_q_norm[j] * (LANES ** -0.5 * LOG2E)
            colgain = jnp.concatenate([jnp.tile(q_gain, heads), jnp.tile(sb_k_norm[j], heads),
                                       jnp.ones((d,), F32)]).reshape(1, 3 * d)
            qkv = _qkv_proj(xt, row(norm_mix[i]), sb_w_in[j].astype(BF16), colgain)
            o = _stick_breaking(qkv.reshape(b, s, 3 * d), heads).reshape(t, d)
            mix = dict(attn_out=o)
            w_out = sb_w_out[j]
        else:
            z = _sg_in_proj(xt, row(norm_mix[i]), sg_w_in[j].astype(BF16), row(sg_v_norm[j]))
            mix = dict(sg=(z, sg_w_s[j], sg_b_s[j].T))
            w_out = sg_w_out[j]
        xt = _moe_and_ple(xt, mix, w_out.astype(BF16), row(norm_ffn[i]), router, expert_w,
                          p[i].reshape(t, -1), ple)
    return xt.reshape(b, s, d)
```

```python
import functools

import jax
import jax.numpy as jnp
from jax import lax
from jax.experimental import pallas as pl
from jax.experimental.pallas import tpu as pltpu

F32 = jnp.float32
BF16 = jnp.bfloat16

LANES = 128
ROW_TILES = 16
HALF_TILES = ROW_TILES // 2
VMEM_LIMIT_BYTES = 56 * 1024 * 1024
EPS = 1e-6
LOG2E = 1.4426950408889634
INV_LN2 = LOG2E
SIGN_BIT = -2 ** 31

N_GROUPS = 4
EXPERTS_PER_GROUP = 8
N_EXPERTS = N_GROUPS * EXPERTS_PER_GROUP
ROUTER_GROUP_LANE0 = N_EXPERTS

TM_PROJ = 512
TM_ROW = 256
TQ = 256
HEADS_PER_STEP = 4
TM_EXPERT = 256


def _cparams(*sem):
    return pltpu.CompilerParams(dimension_semantics=sem, vmem_limit_bytes=VMEM_LIMIT_BYTES)


def _rms(x, g):
    ms = jnp.mean(x * x, axis=-1, keepdims=True)
    return x * lax.rsqrt(ms + EPS) * g


def _qkv_kernel(x_ref, g_ref, w_ref, cg_ref, o_ref, xn_ref, *, n_norm_tiles):
    j = pl.program_id(1)

    @pl.when(j == 0)
    def _():
        xn_ref[...] = _rms(x_ref[...], g_ref[...]).astype(BF16)

    is_norm = j < n_norm_tiles
    acc = jnp.dot(xn_ref[...], w_ref[...], preferred_element_type=F32)
    for h in range(acc.shape[1] // LANES):
        sl = slice(h * LANES, (h + 1) * LANES)
        a = acc[:, sl]
        ms = jnp.mean(a * a, axis=-1, keepdims=True)
        scale = jnp.where(is_norm, lax.rsqrt(ms + EPS), 1.0)
        o_ref[:, sl] = (a * scale * cg_ref[:, sl]).astype(BF16)


def _qkv_proj(x, g, w, colgain, *, tn=1024):
    t, d = x.shape
    n = w.shape[1]
    tm = min(TM_PROJ, t)
    tn = min(tn, d)
    return pl.pallas_call(
        functools.partial(_qkv_kernel, n_norm_tiles=2 * d // tn),
        grid=(t // tm, n // tn),
        in_specs=[
            pl.BlockSpec((tm, d), lambda i, j: (i, 0)),
            pl.BlockSpec((1, d), lambda i, j: (0, 0)),
            pl.BlockSpec((d, tn), lambda i, j: (0, j)),
            pl.BlockSpec((1, tn), lambda i, j: (0, j)),
        ],
        out_specs=pl.BlockSpec((tm, tn), lambda i, j: (i, j)),
        out_shape=jax.ShapeDtypeStruct((t, n), BF16),
        scratch_shapes=[pltpu.VMEM((tm, d), BF16)],
        compiler_params=_cparams("parallel", "arbitrary"),
        name="qkv_proj",
    )(x, g, w, colgain)


def _sg_in_kernel(x_ref, g_ref, w_ref, vg_ref, o_ref, xn_ref):
    j = pl.program_id(1)

    @pl.when(j == 0)
    def _():
        xn_ref[...] = _rms(x_ref[...], g_ref[...]).astype(BF16)

    z = jax.nn.gelu(jnp.dot(xn_ref[...], w_ref[...], preferred_element_type=F32))
    is_v = j == 1
    ms = jnp.mean(z * z, axis=-1, keepdims=True)
    scale = jnp.where(is_v, lax.rsqrt(ms + EPS), 1.0)
    gain = jnp.where(is_v, vg_ref[...], 1.0)
    o_ref[...] = (z * scale * gain).astype(BF16)


def _sg_in_proj(x, g, w, v_gain):
    t, d = x.shape
    tm = min(TM_PROJ, t)
    return pl.pallas_call(
        _sg_in_kernel,
        grid=(t // tm, 2),
        in_specs=[
            pl.BlockSpec((tm, d), lambda i, j: (i, 0)),
            pl.BlockSpec((1, d), lambda i, j: (0, 0)),
            pl.BlockSpec((d, d), lambda i, j: (0, j)),
            pl.BlockSpec((1, d), lambda i, j: (0, 0)),
        ],
        out_specs=pl.BlockSpec((tm, d), lambda i, j: (i, j)),
        out_shape=jax.ShapeDtypeStruct((t, 2 * d), BF16),
        scratch_shapes=[pltpu.VMEM((tm, d), BF16)],
        compiler_params=_cparams("parallel", "arbitrary"),
        name="sg_in_proj",
    )(x, g, w, v_gain)


def _attn_kernel(q_ref, k_ref, v_ref, uu_ref, o_ref, acc_ref, carry_ref, za_ref, zb_ref):
    qi = pl.program_id(2)
    tq = q_ref.shape[0]
    tk = uu_ref.shape[1]
    n_heads = q_ref.shape[1] // LANES
    heads = range(n_heads)
    head = lambda ref_or_val, h: ref_or_val[:, h * LANES:(h + 1) * LANES]
    qs = [head(q_ref, h) for h in heads]

    def key_rows(kj):
        return pl.ds(pl.multiple_of(kj * tk, tk), tk)

    def store_logits(kj, z_out):
        k_all = k_ref[key_rows(kj), :]
        for h in heads:
            z_out[h] = lax.dot_general(qs[h], head(k_all, h), (((1,), (1,)), ((), ())),
                                       preferred_element_type=F32)

    def neg_abs(z):
        return lax.bitcast_convert_type(lax.bitcast_convert_type(z, jnp.int32) | SIGN_BIT, F32)

    def tile(kj, z_in, z_out, causal):
        store_logits(jnp.maximum(kj - 1, 0), z_out)
        v_all = v_ref[key_rows(kj), :]
        cs = []
        for h in heads:
            z = z_in[h]
            sp = jnp.maximum(z, 0.0) + jnp.log(1.0 + jnp.exp2(neg_abs(z))) * INV_LN2
            if causal is not None:
                sp = jnp.where(causal, sp, 0.0)
            cs.append(jnp.dot(sp.astype(BF16), uu_ref[...], preferred_element_type=F32))
        for h in heads:
            w = jnp.exp2(z_in[h] - cs[h])
            if causal is not None:
                w = jnp.where(causal, w, 0.0)
            pv = jnp.dot(w.astype(BF16), head(v_all, h), preferred_element_type=F32)
            carry = carry_ref[h]
            acc_ref[h] += jnp.exp2(-carry) * pv
            carry_ref[h] = carry + cs[h][:, 0:1]

    acc_ref[...] = jnp.zeros_like(acc_ref)
    carry_ref[...] = jnp.zeros_like(carry_ref)
    rows = lax.broadcasted_iota(jnp.int32, (tq, tk), 0)
    cols = lax.broadcasted_iota(jnp.int32, (tq, tk), 1)
    store_logits(qi, za_ref)
    tile(qi, za_ref, zb_ref, cols < rows)

    def pair(kj):
        tile(kj, zb_ref, za_ref, None)
        tile(kj - 1, za_ref, zb_ref, None)

    def quad(p, c):
        kj = qi - 1 - 4 * p
        pair(kj)
        pair(kj - 2)
        return c

    lax.fori_loop(0, qi // 4, quad, 0)
    left = qi % 4

    @pl.when(left >= 2)
    def _():
        pair(left - 1)

    @pl.when(left % 2 == 1)
    def _():
        tile(0, zb_ref, za_ref, None)

    for h in heads:
        o_ref[:, h * LANES:(h + 1) * LANES] = acc_ref[h].astype(o_ref.dtype)


def _stick_breaking(qkv, heads):
    b, s, d3 = qkv.shape
    d = d3 // 3
    tq = min(TQ, s)
    tk = tq
    hw = HEADS_PER_STEP * LANES
    groups = heads // HEADS_PER_STEP
    j = lax.broadcasted_iota(jnp.int32, (tk, tk), 0)
    c = lax.broadcasted_iota(jnp.int32, (tk, tk), 1)
    uu = (j >= c).astype(BF16)
    return pl.pallas_call(
        _attn_kernel,
        grid=(b, groups, s // tq),
        in_specs=[
            pl.BlockSpec((None, tq, hw), lambda bi, h, i: (bi, i, h)),
            pl.BlockSpec((None, s, hw), lambda bi, h, i: (bi, 0, groups + h)),
            pl.BlockSpec((None, s, hw), lambda bi, h, i: (bi, 0, 2 * groups + h)),
            pl.BlockSpec((tk, tk), lambda bi, h, i: (0, 0)),
        ],
        out_specs=pl.BlockSpec((None, tq, hw), lambda bi, h, i: (bi, i, h)),
        out_shape=jax.ShapeDtypeStruct((b, s, d), BF16),
        scratch_shapes=[pltpu.VMEM((HEADS_PER_STEP, tq, LANES), F32),
                        pltpu.VMEM((HEADS_PER_STEP, tq, LANES), F32),
                        pltpu.VMEM((HEADS_PER_STEP, tq, tk), F32),
                        pltpu.VMEM((HEADS_PER_STEP, tq, tk), F32)],
        compiler_params=_cparams("parallel", "parallel", "arbitrary"),
        name="stick_breaking_attention",
    )(qkv, qkv, qkv, uu)


def _split_hi_lo(x):
    hi = x.astype(BF16)
    return hi, (x - hi.astype(F32)).astype(BF16)


def _route_tile(h2, wr_ref, br_ref, ltri_ref, cnt_ref):
    tm = h2.shape[0]
    hi, lo = _split_hi_lo(h2)
    lhs = jnp.concatenate([hi, lo, hi], axis=1)
    logits = jnp.dot(lhs, wr_ref[...], preferred_element_type=F32) + br_ref[...]

    lane = lax.broadcasted_iota(jnp.int32, (tm, LANES), 1)
    neg = jnp.float32(-jnp.inf)
    big = jnp.int32(LANES)

    is_group = (lane >= ROUTER_GROUP_LANE0) & (lane < ROUTER_GROUP_LANE0 + N_GROUPS)
    gl = jnp.where(is_group, logits, neg)
    gmax = jnp.max(gl, axis=-1, keepdims=True)
    g_idx = jnp.min(jnp.where(gl == gmax, lane - ROUTER_GROUP_LANE0, big), axis=-1, keepdims=True)
    g_w = 1.0 / jnp.sum(jnp.where(is_group, jnp.exp(gl - gmax), 0.0), axis=-1, keepdims=True)

    in_group = (lane < N_EXPERTS) & ((lane // EXPERTS_PER_GROUP) == g_idx)
    el = jnp.where(in_group, logits, neg)
    v1 = jnp.max(el, axis=-1, keepdims=True)
    i1 = jnp.min(jnp.where(el == v1, lane, big), axis=-1, keepdims=True)
    el2 = jnp.where(lane == i1, neg, el)
    v2 = jnp.max(el2, axis=-1, keepdims=True)
    i2 = jnp.min(jnp.where(el2 == v2, lane, big), axis=-1, keepdims=True)
    e21 = jnp.exp(v2 - v1)
    den = 1.0 + e21
    w1 = g_w * (1.0 / den)
    w2 = g_w * (e21 / den)

    oh1 = lane == i1
    oh2 = lane == i2
    onehot = (oh1 | oh2).astype(BF16)
    ahead = jnp.dot(ltri_ref[...], onehot, preferred_element_type=F32) + cnt_ref[...]
    r1 = jnp.sum(jnp.where(oh1, ahead, 0.0), axis=-1, keepdims=True)
    r2 = jnp.sum(jnp.where(oh2, ahead, 0.0), axis=-1, keepdims=True)
    cnt_ref[...] += jnp.sum(onehot.astype(F32), axis=0, keepdims=True)

    rec = jnp.where(lane == 0, i1.astype(F32), 0.0)
    rec = jnp.where(lane == 1, i2.astype(F32), rec)
    rec = jnp.where(lane == 2, r1, rec)
    rec = jnp.where(lane == 3, r2, rec)
    rec = jnp.where(lane == 4, w1, rec)
    rec = jnp.where(lane == 5, w2, rec)
    return rec


def _store_chunks(ref, val):
    tm = val.shape[0]
    for c in range(ROW_TILES):
        ref.at[c // HALF_TILES][pl.ds(c % HALF_TILES, tm, stride=HALF_TILES), :] = val[:, c * LANES:(c + 1) * LANES]


def _load_chunk(ref, tm, c):
    return ref.at[c // HALF_TILES][pl.ds(c % HALF_TILES, tm, stride=HALF_TILES), :]


def _halves_from_token_major(ref, val):
    tm = val.shape[0] // ROW_TILES
    v4 = val.reshape(tm, 2, HALF_TILES, LANES)
    for j in range(2):
        ref[j] = v4[:, j].reshape(tm * HALF_TILES, LANES)


def _token_major_from_halves(ref):
    tm = ref.shape[1] // HALF_TILES
    parts = [ref[j].reshape(tm, 1, HALF_TILES, LANES) for j in range(2)]
    return jnp.concatenate(parts, axis=1).reshape(tm * ROW_TILES, LANES)


def _mix_out_epilogue(a, x_ref, wo_ref, gf_ref, wr_ref, br_ref, x1_ref, route_ref, cnt_out_ref,
                      ltri_ref, cnt_ref):
    tm = a.shape[0]

    @pl.when(pl.program_id(0) == 0)
    def _():
        r = lax.broadcasted_iota(jnp.int32, (tm, tm), 0)
        c = lax.broadcasted_iota(jnp.int32, (tm, tm), 1)
        ltri_ref[...] = (c < r).astype(BF16)
        cnt_ref[...] = jnp.zeros_like(cnt_ref)

    x1 = x_ref[...] + jnp.dot(a, wo_ref[...], preferred_element_type=F32)
    x1_ref[...] = x1
    route_ref[...] = _route_tile(_rms(x1, gf_ref[...]), wr_ref, br_ref, ltri_ref, cnt_ref)
    cnt_out_ref[...] = cnt_ref[...]


def _attn_out_kernel(a_ref, x_ref, wo_ref, gf_ref, wr_ref, br_ref,
                     x1_ref, route_ref, cnt_out_ref, ltri_ref, cnt_ref):
    _mix_out_epilogue(a_ref[...], x_ref, wo_ref, gf_ref, wr_ref, br_ref,
                      x1_ref, route_ref, cnt_out_ref, ltri_ref, cnt_ref)


def _sg_out_kernel(u_ref, vn_ref, ws_ref, bs_ref, x_ref, wo_ref, gf_ref, wr_ref, br_ref,
                   x1_ref, route_ref, cnt_out_ref, ltri_ref, cnt_ref, a_ref):
    tm, d = u_ref.shape
    r = lax.broadcasted_iota(jnp.int32, (LANES, LANES), 0)
    c = lax.broadcasted_iota(jnp.int32, (LANES, LANES), 1)
    keep = c <= r
    for g in range(d // LANES):
        sl = slice(g * LANES, (g + 1) * LANES)
        wc = jnp.where(keep, ws_ref[g], 0.0).astype(BF16)
        bias = bs_ref[:, g:g + 1]
        for ch in range(tm // LANES):
            rs = slice(ch * LANES, (ch + 1) * LANES)
            mixed = jnp.dot(wc, vn_ref[rs, sl], preferred_element_type=F32) + bias
            a_ref[rs, sl] = (u_ref[rs, sl].astype(F32) * mixed).astype(BF16)
    _mix_out_epilogue(a_ref[...], x_ref, wo_ref, gf_ref, wr_ref, br_ref,
                      x1_ref, route_ref, cnt_out_ref, ltri_ref, cnt_ref)


def _mix_out(x, w_out, g_ffn, w_router, b_router, *, attn_out=None, sg=None):
    t, d = x.shape
    tm = min(TM_ROW, t)
    row_spec = pl.BlockSpec((tm, d), lambda i: (i, 0))
    const = lambda shape: pl.BlockSpec(shape, lambda i: (0,) * len(shape))
    common_in = [row_spec, const((d, d)), const((1, d)), const(w_router.shape), const((1, LANES))]
    common_args = (x, w_out, g_ffn, w_router, b_router)
    out_specs = [row_spec, pl.BlockSpec((tm, LANES), lambda i: (i, 0)), const((1, LANES))]
    out_shape = [
        jax.ShapeDtypeStruct((t, d), F32),
        jax.ShapeDtypeStruct((t, LANES), F32),
        jax.ShapeDtypeStruct((1, LANES), F32),
    ]
    scratch = [pltpu.VMEM((tm, tm), BF16), pltpu.VMEM((1, LANES), F32)]
    if attn_out is not None:
        return pl.pallas_call(
            _attn_out_kernel,
            grid=(t // tm,),
            in_specs=[row_spec] + common_in,
            out_specs=out_specs, out_shape=out_shape, scratch_shapes=scratch,
            compiler_params=_cparams("arbitrary"),
            name="attn_out_router",
        )(attn_out, *common_args)
    z, w_s, b_s_t = sg
    return pl.pallas_call(
        _sg_out_kernel,
        grid=(t // tm,),
        in_specs=[
            pl.BlockSpec((tm, d), lambda i: (i, 0)),
            pl.BlockSpec((tm, d), lambda i: (i, 1)),
            const(w_s.shape),
            const(b_s_t.shape),
        ] + common_in,
        out_specs=out_specs, out_shape=out_shape,
        scratch_shapes=scratch + [pltpu.VMEM((tm, d), BF16)],
        compiler_params=_cparams("arbitrary"),
        name="sg_out_router",
    )(z, z, w_s, b_s_t, *common_args)


def _row_slice(ref, row):
    return ref.at[pl.ds(pl.multiple_of(row * ROW_TILES, ROW_TILES), ROW_TILES)]


def _for_each_token(n, fn, unroll=8):
    def group(o, carry):
        for u in range(unroll):
            fn(o * unroll + u)
        return carry
    lax.fori_loop(0, n // unroll, group, 0)


def _dispatch_kernel(pos_ref, pad_start_ref, pad_n_ref, n_used_ref, x1_ref, g_ref, xs_ref,
                     f32_ref, stage_ref, zero_ref, sems, pad_sem):
    i = pl.program_id(0)
    n_steps = pl.num_programs(0)
    td = x1_ref.shape[0]
    slot = i % 2

    def wait_slot(s):
        for _ in range(2):
            pltpu.make_async_copy(stage_ref.at[s], xs_ref.at[pl.ds(0, td * ROW_TILES)], sems.at[s]).wait()

    @pl.when(i == 0)
    def _():
        zero_ref[...] = jnp.zeros_like(zero_ref)
        tile_rows = zero_ref.shape[0]
        n_tiles = xs_ref.shape[0] // tile_rows

        def pad_copy(e, r):
            return pltpu.make_async_copy(zero_ref.at[pl.ds(0, ROW_TILES)],
                                         _row_slice(xs_ref, pad_start_ref[e] + r), pad_sem)

        def tile_copy(j):
            return pltpu.make_async_copy(
                zero_ref, xs_ref.at[pl.ds(pl.multiple_of(j * tile_rows, tile_rows), tile_rows)], pad_sem)

        def start_all(e, carry):
            lax.fori_loop(0, pad_n_ref[e], lambda r, c: (pad_copy(e, r).start(), c)[1], 0)
            return carry

        def wait_all(e, carry):
            lax.fori_loop(0, pad_n_ref[e], lambda r, c: (pad_copy(e, r).wait(), c)[1], 0)
            return carry

        lax.fori_loop(0, pad_n_ref.shape[0], start_all, 0)
        lax.fori_loop(n_used_ref[0], n_tiles, lambda j, c: (tile_copy(j).start(), c)[1], 0)
        lax.fori_loop(0, pad_n_ref.shape[0], wait_all, 0)
        lax.fori_loop(n_used_ref[0], n_tiles, lambda j, c: (tile_copy(j).wait(), c)[1], 0)

    @pl.when(i >= 2)
    def _():
        wait_slot(slot)

    _store_chunks(f32_ref, _rms(x1_ref[...], g_ref[...]))
    stage_ref[slot] = _token_major_from_halves(f32_ref).astype(BF16)

    def issue(t):
        src = stage_ref.at[slot, pl.ds(pl.multiple_of(t * ROW_TILES, ROW_TILES), ROW_TILES)]
        for k in range(2):
            pltpu.make_async_copy(src, _row_slice(xs_ref, pos_ref[0, 0, k * td + t]),
                                  sems.at[slot]).start(priority=k)

    _for_each_token(td, issue)

    @pl.when(i == n_steps - 1)
    def _():
        wait_slot(slot)

    @pl.when((i == n_steps - 1) & (i >= 1))
    def _():
        wait_slot(1 - slot)


def _dispatch(x1, g_ffn, pos, pad_start, pad_n, n_used, n_sorted_rows):
    t, d = x1.shape
    td = min(TM_ROW, t)
    pos3 = pos.reshape(t // td, 1, 2 * td)
    smem = pl.BlockSpec(memory_space=pltpu.SMEM)
    return pl.pallas_call(
        _dispatch_kernel,
        grid=(t // td,),
        in_specs=[
            pl.BlockSpec((1, 1, 2 * td), lambda i: (i, 0, 0), memory_space=pltpu.SMEM),
            smem, smem, smem,
            pl.BlockSpec((td, d), lambda i: (i, 0)),
            pl.BlockSpec((1, d), lambda i: (0, 0)),
        ],
        out_specs=pl.BlockSpec(memory_space=pl.ANY),
        out_shape=jax.ShapeDtypeStruct((n_sorted_rows * ROW_TILES, LANES), BF16),
        scratch_shapes=[
            pltpu.VMEM((2, td * HALF_TILES, LANES), F32),
            pltpu.VMEM((2, td * ROW_TILES, LANES), BF16),
            pltpu.VMEM((TM_EXPERT * ROW_TILES, LANES), BF16),
            pltpu.SemaphoreType.DMA((2,)),
            pltpu.SemaphoreType.DMA(()),
        ],
        compiler_params=_cparams("arbitrary"),
        name="moe_dispatch",
    )(pos3, pad_start, pad_n, n_used, x1, g_ffn)


def _expert_kernel(tile_expert_ref, n_used_ref, xs_ref, wg_ref, wu_ref, wd_ref, ys_ref,
                   f32_ref, a_ref, wgu_bf_ref, wd_bf_ref):
    i = pl.program_id(0)
    tm, d = a_ref.shape
    n_row = d // LANES
    f = wd_ref.shape[0]

    @pl.when((i == 0) | (tile_expert_ref[i] != tile_expert_ref[jnp.maximum(i - 1, 0)]))
    def _():
        wgu_bf_ref[:, :f] = wg_ref[...].astype(BF16)
        wgu_bf_ref[:, f:] = wu_ref[...].astype(BF16)
        wd_bf_ref[...] = wd_ref[...].astype(BF16)

    @pl.when(i < n_used_ref[0])
    def _():
        _halves_from_token_major(f32_ref, xs_ref[...].astype(F32))
        for c in range(n_row):
            a_ref[:, c * LANES:(c + 1) * LANES] = _load_chunk(f32_ref, tm, c).astype(BF16)
        gu = jnp.dot(a_ref[...], wgu_bf_ref[...], preferred_element_type=F32)
        hidden = (jax.nn.silu(gu[:, :f]) * gu[:, f:]).astype(BF16)
        _store_chunks(f32_ref, jnp.dot(hidden, wd_bf_ref[...], preferred_element_type=F32))
        ys_ref[...] = _token_major_from_halves(f32_ref).astype(BF16)

    @pl.when(i >= n_used_ref[0])
    def _():
        ys_ref[...] = jnp.zeros_like(ys_ref)


def _experts(xs, layer, w_gate, w_up, w_down, tile_expert, n_used):
    _, e, d, f = w_gate.shape
    n_rows = xs.shape[0] // ROW_TILES
    tm = TM_EXPERT
    n_tiles = n_rows // tm
    return pl.pallas_call(
        _expert_kernel,
        grid_spec=pltpu.PrefetchScalarGridSpec(
            num_scalar_prefetch=2,
            grid=(n_tiles,),
            in_specs=[
                pl.BlockSpec((tm * ROW_TILES, LANES), lambda i, te, nu: (jnp.minimum(i, nu[0] - 1), 0)),
                pl.BlockSpec((None, None, d, f), lambda i, te, nu: (layer, te[i], 0, 0)),
                pl.BlockSpec((None, None, d, f), lambda i, te, nu: (layer, te[i], 0, 0)),
                pl.BlockSpec((None, None, f, d), lambda i, te, nu: (layer, te[i], 0, 0)),
            ],
            out_specs=pl.BlockSpec((tm * ROW_TILES, LANES), lambda i, te, nu: (i, 0)),
            scratch_shapes=[pltpu.VMEM((2, tm * HALF_TILES, LANES), F32), pltpu.VMEM((tm, d), BF16),
                            pltpu.VMEM((d, 2 * f), BF16), pltpu.VMEM((f, d), BF16)],
        ),
        out_shape=jax.ShapeDtypeStruct(xs.shape, BF16),
        compiler_params=_cparams("arbitrary"),
        name="moe_experts",
    )(tile_expert, n_used, xs, w_gate, w_up, w_down)


def _ple_kernel(pos_ref, pos_next_ref, x1_ref, route_ref, p_ref, gin_ref, wg_ref, wp_ref, gout_ref, ys_ref,
                o_ref, ybuf_ref, f32_ref, x2_ref, sems):
    i = pl.program_id(0)
    n_steps = pl.num_programs(0)
    tm, d = x1_ref.shape
    n_row = d // LANES
    slot = i % 2

    def gather(p_ref_, s):
        def issue(t):
            for k in range(2):
                dst = ybuf_ref.at[s, k, pl.ds(pl.multiple_of(t * ROW_TILES, ROW_TILES), ROW_TILES)]
                pltpu.make_async_copy(_row_slice(ys_ref, p_ref_[0, 0, k * tm + t]), dst,
                                      sems.at[s]).start(priority=k)
        _for_each_token(tm, issue)

    @pl.when(i == 0)
    def _():
        gather(pos_ref, 0)

    @pl.when(i + 1 < n_steps)
    def _():
        gather(pos_next_ref, 1 - slot)

    for k in range(2):
        pltpu.make_async_copy(ys_ref.at[pl.ds(0, tm * ROW_TILES)], ybuf_ref.at[slot, k], sems.at[slot]).wait()

    w = (route_ref[:, 4:5], route_ref[:, 5:6])
    for k in range(2):
        _halves_from_token_major(f32_ref.at[k], ybuf_ref[slot, k].astype(F32))
    for c in range(n_row):
        sl = slice(c * LANES, (c + 1) * LANES)
        y0 = _load_chunk(f32_ref.at[0], tm, c)
        y1 = _load_chunk(f32_ref.at[1], tm, c)
        x2_ref[:, sl] = x1_ref[:, sl] + w[0] * y0 + w[1] * y1
    x2 = x2_ref[...]
    gate = jax.nn.sigmoid(jnp.dot(_rms(x2, gin_ref[...]).astype(BF16), wg_ref[...],
                                  preferred_element_type=F32))
    e = jnp.dot(p_ref[...].astype(BF16), wp_ref[...], preferred_element_type=F32) * gate
    o_ref[...] = x2 + _rms(e, gout_ref[...])


def _ple(x1, ys, pos, route, p, g_in, w_gate, w_proj, g_out):
    t, d = x1.shape
    tm = min(TM_ROW, t)
    n_steps = t // tm
    pd = p.shape[1]
    pos3 = pos.reshape(n_steps, 1, 2 * tm)
    row_spec = pl.BlockSpec((tm, d), lambda i: (i, 0))
    const = lambda shape: pl.BlockSpec(shape, lambda i: (0,) * len(shape))
    return pl.pallas_call(
        _ple_kernel,
        grid=(n_steps,),
        in_specs=[
            pl.BlockSpec((1, 1, 2 * tm), lambda i: (i, 0, 0), memory_space=pltpu.SMEM),
            pl.BlockSpec((1, 1, 2 * tm), lambda i: (jnp.minimum(i + 1, n_steps - 1), 0, 0),
                         memory_space=pltpu.SMEM),
            row_spec,
            pl.BlockSpec((tm, LANES), lambda i: (i, 0)),
            pl.BlockSpec((tm, pd), lambda i: (i, 0)),
            const((1, d)), const((d, d)), const((pd, d)), const((1, d)),
            pl.BlockSpec(memory_space=pl.ANY),
        ],
        out_specs=row_spec,
        out_shape=jax.ShapeDtypeStruct((t, d), F32),
        scratch_shapes=[
            pltpu.VMEM((2, 2, tm * ROW_TILES, LANES), BF16),
            pltpu.VMEM((2, 2, tm * HALF_TILES, LANES), F32),
            pltpu.VMEM((tm, d), F32),
            pltpu.SemaphoreType.DMA((2,)),
        ],
        compiler_params=_cparams("arbitrary"),
        name="moe_combine_ple",
    )(pos3, pos3, x1, route, p, g_in, w_gate, w_proj, g_out, ys)


def _router_weights(w_group, b_group, w_expert, b_expert):
    d = w_group.shape[0]
    pad = LANES - N_EXPERTS - N_GROUPS
    w = jnp.concatenate([w_expert, w_group, jnp.zeros((d, pad), F32)], axis=1)
    b = jnp.concatenate([b_expert, b_group, jnp.zeros((pad,), F32)]).reshape(1, LANES)
    w_hi = w.astype(BF16)
    w_lo = (w - w_hi.astype(F32)).astype(BF16)
    return jnp.concatenate([w_hi, w_hi, w_lo], axis=0), b


def _sorted_layout(route, counts, t):
    tm = TM_EXPERT
    n_tiles = (2 * t + N_EXPERTS * (tm - 1)) // tm + 1
    ids = route[:, 0:4].T.astype(jnp.int32)
    eid, rank = ids[0:2], ids[2:4]
    cnt = counts[0, :N_EXPERTS].astype(jnp.int32)
    padded = ((cnt + tm - 1) // tm) * tm
    ends = jnp.cumsum(padded)
    offs = ends - padded
    one_hot = eid[None] == jnp.arange(N_EXPERTS, dtype=jnp.int32)[:, None, None]
    pos = rank + jnp.sum(jnp.where(one_hot, offs[:, None, None], 0), axis=0)
    pos = pos.reshape(2, t // min(TM_ROW, t), -1).transpose(1, 0, 2)
    pos = pos.reshape(pos.shape[0], 1, -1)
    tile_start = jnp.arange(n_tiles, dtype=jnp.int32) * tm
    n_used = (ends[-1] // tm).astype(jnp.int32)
    probe = jnp.minimum(tile_start, ends[-1] - tm)
    te = jnp.sum((probe[:, None] >= ends[None, :]).astype(jnp.int32), axis=1)
    return dict(pos=pos, tile_expert=te, n_used=n_used.reshape(1), pad_start=offs + cnt,
                pad_n=padded - cnt, n_sorted=n_tiles * tm)


def _moe_and_ple(x, mix_out_kwargs, w_out, g_ffn, router, expert_w, p_i, ple):
    t = x.shape[0]
    x1, route, counts = _mix_out(x, w_out, g_ffn, *router, **mix_out_kwargs)
    lay = _sorted_layout(route, counts, t)
    xs = _dispatch(x1, g_ffn, lay["pos"], lay["pad_start"], lay["pad_n"], lay["n_used"], lay["n_sorted"])
    ys = _experts(xs, *expert_w, lay["tile_expert"], lay["n_used"])
    return _ple(x1, ys, lay["pos"], route, p_i, *ple)


def kernel(x, p, norm_mix, norm_ffn, sb_w_in, sb_q_norm, sb_k_norm, sb_w_out, sg_w_in, sg_v_norm, sg_w_s, sg_b_s, sg_w_out, moe_w_group, moe_b_group, moe_w_expert, moe_b_expert, moe_w_gate, moe_w_up, moe_w_down, ple_norm_in, ple_w_gate, ple_w_proj, ple_norm_out):
    b, s, d = x.shape
    depth = norm_mix.shape[0]
    heads = d // LANES
    assert d == ROW_TILES * LANES
    t = b * s
    xt = x.reshape(t, d)
    row = lambda v: v.reshape(1, -1)
    for i in range(depth):
        j = i // 2
        router = _router_weights(moe_w_group[i], moe_b_group[i], moe_w_expert[i], moe_b_expert[i])
        expert_w = (i, moe_w_gate, moe_w_up, moe_w_down)
        ple = (row(ple_norm_in[i]), ple_w_gate[i].astype(BF16), ple_w_proj[i].astype(BF16), row(ple_norm_out[i]))
        if i % 2 == 0:
            q_gain = sb_q_norm[j] * (LANES ** -0.5 * LOG2E)
            colgain = jnp.concatenate([jnp.tile(q_gain, heads), jnp.tile(sb_k_norm[j], heads),
                                       jnp.ones((d,), F32)]).reshape(1, 3 * d)
            qkv = _qkv_proj(xt, row(norm_mix[i]), sb_w_in[j].astype(BF16), colgain)
            o = _stick_breaking(qkv.reshape(b, s, 3 * d), heads).reshape(t, d)
            mix = dict(attn_out=o)
            w_out = sb_w_out[j]
        else:
            z = _sg_in_proj(xt, row(norm_mix[i]), sg_w_in[j].astype(BF16), row(sg_v_norm[j]))
            mix = dict(sg=(z, sg_w_s[j], sg_b_s[j].T))
            w_out = sg_w_out[j]
        xt = _moe_and_ple(xt, mix, w_out.astype(BF16), row(norm_ffn[i]), router, expert_w,
                          p[i].reshape(t, -1), ple)
    return xt.reshape(b, s, d)
```
